```python
import jax, jax.numpy as jnp
from jax import lax
import numpy as np

D_MODEL = 1024
BATCH = 8
SEQ = 4096
DEPTH = 4

N_MIXERS = 2
N_DN_LAYERS = (DEPTH + 1) // 2
N_MLA_LAYERS = DEPTH // 2
DN_NK = 8
DN_NV = 16
DN_DK = 128
DN_DV = 128
DN_K_DIM = DN_NK * DN_DK
DN_V_DIM = DN_NV * DN_DV
DN_CONV_CH = 2 * DN_K_DIM + DN_V_DIM
DN_IN = DN_CONV_CH + DN_V_DIM + 4 * DN_NV
DN_CONV_W = 5
DN_CONV_PAD = DN_CONV_W // 2
DN_CHUNK = 64
MLA_H = 8
MLA_NOPE = 128
MLA_ROPE = 64
MLA_V = 128
MLA_QR = 384
MLA_KVR = 256
MLA_IN = MLA_QR + MLA_KVR + MLA_ROPE
MLA_ROPE_THETA = 10000.0
MLA_QBLOCK = 128
N_EXPERTS = 16
N_GROUPS = 4
EXPERTS_PER_GROUP = N_EXPERTS // N_GROUPS
TOP_K = 2
D_FF_EXPERT = 256
DEEPNORM_ALPHA = (2 * DEPTH) ** 0.25
DEEPNORM_BETA = (8 * DEPTH) ** -0.25
LN_EPS = 1e-5
RMS_EPS = 1e-6

kernel_name = "hybrid_deltanet_mla_grouped_moe_deepnorm"


def _layernorm(x, g, b):
    xf = x.astype(jnp.float32)
    mu = jnp.mean(xf, -1, keepdims=True)
    var = jnp.mean(jnp.square(xf - mu), -1, keepdims=True)
    y = (xf - mu) * lax.rsqrt(var + LN_EPS) * g.astype(jnp.float32) + b.astype(jnp.float32)
    return y.astype(x.dtype)


def _rmsnorm_f32(x, w):
    xf = x.astype(jnp.float32)
    return xf * lax.rsqrt(jnp.mean(xf * xf, -1, keepdims=True) + RMS_EPS) * w.astype(jnp.float32)


def _l2norm(x):
    return x * lax.rsqrt(jnp.sum(x * x, -1, keepdims=True) + RMS_EPS)


def _short_conv(x, w):
    return lax.conv_general_dilated(x, w[:, None, :], window_strides=(1,),
                                    padding=((DN_CONV_PAD, DN_CONV_PAD),),
                                    dimension_numbers=('NWC', 'WIO', 'NWC'),
                                    feature_group_count=x.shape[-1])


def _gated_delta_rule_chunked(q, k, v, g, beta):
    bsz, nh, seqlen, dk = q.shape
    dv = v.shape[-1]
    n = seqlen // DN_CHUNK

    def chunks(t):
        return jnp.moveaxis(t.reshape(bsz, nh, n, DN_CHUNK, *t.shape[3:]), 2, 0)

    q, k, v, g, beta = (chunks(t) for t in (q, k, v, g, beta))
    g = jnp.cumsum(g, axis=-1)
    idx = jnp.arange(DN_CHUNK)
    lower = idx[:, None] >= idx[None, :]
    strict = idx[:, None] > idx[None, :]
    decay = jnp.exp(jnp.where(lower, g[..., :, None] - g[..., None, :], -jnp.inf))
    k_beta = k * beta[..., None]
    a = jnp.where(strict, jnp.einsum('nbhcd,nbhsd->nbhcs', k_beta, k) * decay, 0.0)
    eye = jnp.eye(DN_CHUNK, dtype=q.dtype)
    t_inv = lax.linalg.triangular_solve(a + eye, jnp.broadcast_to(eye, a.shape),
                                        left_side=True, lower=True, unit_diagonal=True)
    u = jnp.einsum('nbhcs,nbhsd->nbhcd', t_inv, v * beta[..., None])
    w = jnp.einsum('nbhcs,nbhsd->nbhcd', t_inv, k_beta * jnp.exp(g)[..., None])
    qk = jnp.einsum('nbhcd,nbhsd->nbhcs', q, k) * decay

    def step(state, inp):
        q_c, k_c, u_c, w_c, g_c, qk_c = inp
        v_new = u_c - jnp.einsum('bhcd,bhde->bhce', w_c, state)
        o_c = (jnp.einsum('bhcd,bhde->bhce', q_c * jnp.exp(g_c)[..., None], state)
               + jnp.einsum('bhcs,bhse->bhce', qk_c, v_new))
        g_last = g_c[..., -1:]
        state = (state * jnp.exp(g_last)[..., None]
                 + jnp.einsum('bhcd,bhce->bhde', k_c * jnp.exp(g_last - g_c)[..., None], v_new))
        return state, o_c

    state0 = jnp.zeros((bsz, nh, dk, dv), q.dtype)
    _, o = lax.scan(step, state0, (q, k, u, w, g, qk))
    return jnp.moveaxis(o, 0, 2).reshape(bsz, nh, seqlen, dv)


def _gated_deltanet_mixer(x, w_in, conv_w, a_log, dt_bias, norm_w, w_out):
    bsz, seqlen, _ = x.shape
    h = x @ w_in
    qkv, z, b, a = jnp.split(h, [DN_CONV_CH, DN_CONV_CH + DN_V_DIM, DN_CONV_CH + DN_V_DIM + 2 * DN_NV], axis=-1)
    qkv = jax.nn.silu(_short_conv(qkv, conv_w))
    q, k, v = jnp.split(qkv.astype(jnp.float32), [DN_K_DIM, 2 * DN_K_DIM], axis=-1)
    rep = DN_NV // DN_NK
    q = jnp.repeat(_l2norm(q.reshape(bsz, seqlen, DN_NK, DN_DK)) * (DN_DK ** -0.5), rep, axis=2)
    k = jnp.repeat(_l2norm(k.reshape(bsz, seqlen, DN_NK, DN_DK)), rep, axis=2)
    v = v.reshape(bsz, seqlen, DN_NV, DN_DV)
    q, k, v = (t.transpose(0, 2, 1, 3) for t in (q, k, v))
    beta = jax.nn.sigmoid(b.astype(jnp.float32).reshape(bsz, seqlen, 2, DN_NV))
    g = -jnp.exp(a_log.astype(jnp.float32)) * jax.nn.softplus(
        a.astype(jnp.float32).reshape(bsz, seqlen, 2, DN_NV) + dt_bias.astype(jnp.float32))
    o = jnp.zeros_like(v)
    for d in range(2):
        g_d = g[:, :, d].transpose(0, 2, 1)
        b_d = beta[:, :, d].transpose(0, 2, 1)
        if d == 0:
            o = o + _gated_delta_rule_chunked(q, k, v, g_d, b_d)
        else:
            fl = lambda t: jnp.flip(t, axis=2)
            o = o + fl(_gated_delta_rule_chunked(fl(q), fl(k), fl(v), fl(g_d), fl(b_d)))
    o = o.transpose(0, 2, 1, 3)
    o = _rmsnorm_f32(o, norm_w) * jax.nn.silu(z.astype(jnp.float32).reshape(bsz, seqlen, DN_NV, DN_DV))
    return o.reshape(bsz, seqlen, DN_V_DIM).astype(x.dtype) @ w_out


def _rope_tables(positions):
    inv_freq = 1.0 / (MLA_ROPE_THETA ** (jnp.arange(0, MLA_ROPE, 2, dtype=jnp.float32) / MLA_ROPE))
    ang = positions.astype(jnp.float32)[..., None] * inv_freq
    return jnp.cos(ang), jnp.sin(ang)


def _apply_rope(t, cos, sin):
    tf = t.astype(jnp.float32)
    t1, t2 = jnp.split(tf, 2, axis=-1)
    return jnp.concatenate([t1 * cos - t2 * sin, t2 * cos + t1 * sin], -1).astype(t.dtype)


def _block_attention(q_nope, q_rope, k_nope, k_rope, v):
    bsz, nh, seqlen, _ = q_nope.shape
    scale = (MLA_NOPE + MLA_ROPE) ** -0.5

    def blk(i):
        s = i * MLA_QBLOCK
        qn = lax.dynamic_slice_in_dim(q_nope, s, MLA_QBLOCK, axis=2)
        qr = lax.dynamic_slice_in_dim(q_rope, s, MLA_QBLOCK, axis=2)
        sc = (jnp.einsum('bhqd,bhkd->bhqk', qn, k_nope)
              + jnp.einsum('bhqd,bkd->bhqk', qr, k_rope)).astype(jnp.float32) * scale
        p = jax.nn.softmax(sc, axis=-1)
        return jnp.einsum('bhqk,bhkd->bhqd', p.astype(v.dtype), v)

    o = lax.map(blk, jnp.arange(seqlen // MLA_QBLOCK))
    return o.transpose(1, 2, 0, 3, 4).reshape(bsz, nh, seqlen, MLA_V)


def _mla_mixer(x, cos, sin, w_in, q_norm_w, w_qb, kv_norm_w, w_kvb, w_out):
    bsz, seqlen, _ = x.shape
    h = x @ w_in
    c_q, c_kv, k_rope = jnp.split(h, [MLA_QR, MLA_QR + MLA_KVR], axis=-1)
    q = (_rmsnorm_f32(c_q, q_norm_w).astype(x.dtype) @ w_qb).reshape(bsz, seqlen, MLA_H, MLA_NOPE + MLA_ROPE)
    kv = (_rmsnorm_f32(c_kv, kv_norm_w).astype(x.dtype) @ w_kvb).reshape(bsz, seqlen, MLA_H, MLA_NOPE + MLA_V)
    q_nope, q_rope = q[..., :MLA_NOPE], q[..., MLA_NOPE:]
    k_nope, v = kv[..., :MLA_NOPE], kv[..., MLA_NOPE:]
    q_rope = _apply_rope(q_rope, cos[:, :, None], sin[:, :, None])
    k_rope = _apply_rope(k_rope, cos, sin)
    q_nope, q_rope, k_nope, v = (t.transpose(0, 2, 1, 3) for t in (q_nope, q_rope, k_nope, v))
    o = _block_attention(q_nope, q_rope, k_nope, k_rope, v)
    return o.transpose(0, 2, 1, 3).reshape(bsz, seqlen, MLA_H * MLA_V) @ w_out


def _grouped_moe(x, router_w, router_bias, w_gate, w_up, w_down):
    bsz, seqlen, d = x.shape
    t = x.reshape(-1, d)
    scores = jax.nn.sigmoid((t @ router_w).astype(jnp.float32))
    sel = (scores + router_bias.astype(jnp.float32)).reshape(-1, N_GROUPS, EXPERTS_PER_GROUP)
    group_score = jnp.sum(lax.top_k(sel, TOP_K)[0], axis=-1)
    _, gidx = lax.top_k(group_score, 1)
    in_group = jax.nn.one_hot(gidx[:, 0], N_GROUPS, dtype=jnp.bool_)[:, :, None]
    masked = jnp.where(in_group, sel, -jnp.inf).reshape(-1, N_EXPERTS)
    _, eidx = lax.top_k(masked, TOP_K)
    wts = jnp.take_along_axis(scores, eidx, axis=-1)
    wts = wts / jnp.sum(wts, -1, keepdims=True)
    gates = jnp.sum(jax.nn.one_hot(eidx, N_EXPERTS, dtype=jnp.float32) * wts[..., None], axis=1)
    hid = jax.nn.silu(jnp.einsum('td,edf->tef', t, w_gate)) * jnp.einsum('td,edf->tef', t, w_up)
    y = jnp.einsum('tef,efd->td', hid * gates.astype(hid.dtype)[..., None], w_down)
    return y.reshape(bsz, seqlen, d)


def setup_inputs(seed: int = 0) -> dict:
    key = jax.random.key(seed)
    ks = jax.random.split(key, 24)
    nrm = lambda k, shape, s: jax.random.normal(k, shape, jnp.float32) * s
    D = D_MODEL
    x = jax.random.normal(ks[0], (BATCH, SEQ, D), jnp.float32)
    positions = (jnp.arange(SEQ, dtype=jnp.int32)[None, :]
                 + jax.random.randint(ks[1], (BATCH, 1), 0, SEQ, dtype=jnp.int32))
    dn_w_in = nrm(ks[2], (N_DN_LAYERS, D, DN_IN), D ** -0.5)
    dn_conv_w = nrm(ks[3], (N_DN_LAYERS, DN_CONV_W, DN_CONV_CH), DN_CONV_W ** -0.5)
    dn_a_log = jnp.log(jax.random.uniform(ks[4], (N_DN_LAYERS, 2, DN_NV), jnp.float32, 1.0, 16.0))
    dt = jnp.exp(jax.random.uniform(ks[5], (N_DN_LAYERS, 2, DN_NV), jnp.float32, np.log(1e-3), np.log(1e-1)))
    dn_dt_bias = dt + jnp.log(-jnp.expm1(-dt))
    dn_norm_w = 1.0 + nrm(ks[6], (N_DN_LAYERS, DN_DV), 0.02)
    dn_w_out = nrm(ks[7], (N_DN_LAYERS, DN_V_DIM, D), DN_V_DIM ** -0.5 * DEEPNORM_BETA)
    mla_w_in = nrm(ks[8], (N_MLA_LAYERS, D, MLA_IN), D ** -0.5)
    mla_q_norm = 1.0 + nrm(ks[9], (N_MLA_LAYERS, MLA_QR), 0.02)
    mla_w_qb = nrm(ks[10], (N_MLA_LAYERS, MLA_QR, MLA_H * (MLA_NOPE + MLA_ROPE)), MLA_QR ** -0.5)
    mla_kv_norm = 1.0 + nrm(ks[11], (N_MLA_LAYERS, MLA_KVR), 0.02)
    mla_w_kvb = nrm(ks[12], (N_MLA_LAYERS, MLA_KVR, MLA_H * (MLA_NOPE + MLA_V)), MLA_KVR ** -0.5)
    mla_w_out = nrm(ks[13], (N_MLA_LAYERS, MLA_H * MLA_V, D), (MLA_H * MLA_V) ** -0.5 * DEEPNORM_BETA)
    ln1_g = 1.0 + nrm(ks[14], (DEPTH, D), 0.02)
    ln1_b = nrm(ks[15], (DEPTH, D), 0.02)
    ln2_g = 1.0 + nrm(ks[16], (DEPTH, D), 0.02)
    ln2_b = nrm(ks[17], (DEPTH, D), 0.02)
    router_w = nrm(ks[18], (D, N_EXPERTS), D ** -0.5)
    router_bias = nrm(ks[19], (N_EXPERTS,), 0.01)
    moe_w_gate = nrm(ks[20], (DEPTH, N_EXPERTS, D, D_FF_EXPERT), D ** -0.5)
    moe_w_up = nrm(ks[21], (DEPTH, N_EXPERTS, D, D_FF_EXPERT), D ** -0.5)
    moe_w_down = nrm(ks[22], (DEPTH, N_EXPERTS, D_FF_EXPERT, D), D_FF_EXPERT ** -0.5 * DEEPNORM_BETA)
    return {"x": x, "positions": positions,
            "dn_w_in": dn_w_in, "dn_conv_w": dn_conv_w, "dn_a_log": dn_a_log, "dn_dt_bias": dn_dt_bias,
            "dn_norm_w": dn_norm_w, "dn_w_out": dn_w_out,
            "mla_w_in": mla_w_in, "mla_q_norm": mla_q_norm, "mla_w_qb": mla_w_qb,
            "mla_kv_norm": mla_kv_norm, "mla_w_kvb": mla_w_kvb, "mla_w_out": mla_w_out,
            "ln1_g": ln1_g, "ln1_b": ln1_b, "ln2_g": ln2_g, "ln2_b": ln2_b,
            "router_w": router_w, "router_bias": router_bias,
            "moe_w_gate": moe_w_gate, "moe_w_up": moe_w_up, "moe_w_down": moe_w_down}


def reference(x, positions, dn_w_in, dn_conv_w, dn_a_log, dn_dt_bias, dn_norm_w, dn_w_out,
              mla_w_in, mla_q_norm, mla_w_qb, mla_kv_norm, mla_w_kvb, mla_w_out,
              ln1_g, ln1_b, ln2_g, ln2_b, router_w, router_bias, moe_w_gate, moe_w_up, moe_w_down):
    cos, sin = _rope_tables(positions)
    for i in range(DEPTH):
        j = i // N_MIXERS
        if i % N_MIXERS == 0:
            y = _gated_deltanet_mixer(x, dn_w_in[j], dn_conv_w[j], dn_a_log[j], dn_dt_bias[j],
                                      dn_norm_w[j], dn_w_out[j])
        else:
            y = _mla_mixer(x, cos, sin, mla_w_in[j], mla_q_norm[j], mla_w_qb[j],
                           mla_kv_norm[j], mla_w_kvb[j], mla_w_out[j])
        x = _layernorm(DEEPNORM_ALPHA * x + y.astype(x.dtype), ln1_g[i], ln1_b[i])
        y = _grouped_moe(x, router_w, router_bias, moe_w_gate[i], moe_w_up[i], moe_w_down[i])
        x = _layernorm(DEEPNORM_ALPHA * x + y.astype(x.dtype), ln2_g[i], ln2_b[i])
    return x
```

```python
import functools

import jax
import jax.numpy as jnp
from jax import lax
from jax.experimental import pallas as pl
from jax.experimental.pallas import tpu as pltpu

F32 = jnp.float32
BF16 = jnp.bfloat16

DEPTH = 4
DN_NK = 8
DN_NV = 16
DN_DK = 128
DN_DV = 128
DN_K_DIM = DN_NK * DN_DK
DN_V_DIM = DN_NV * DN_DV
DN_CONV_CH = 2 * DN_K_DIM + DN_V_DIM
DN_CONV_W = 5
DN_CHUNK = 64
MLA_H = 8
MLA_NOPE = 128
MLA_ROPE = 64
MLA_V = 128
MLA_QR = 384
MLA_KVR = 256
MLA_ROPE_THETA = 10000.0
N_EXPERTS = 16
N_GROUPS = 4
EXPERTS_PER_GROUP = N_EXPERTS // N_GROUPS
D_FF_EXPERT = 256
DEEPNORM_ALPHA = (2 * DEPTH) ** 0.25
LN_EPS = 1e-5
RMS_EPS = 1e-6

LANES = 128
VMEM_LIMIT = 48 * 1024 * 1024


def _cparams(*sems):
    return pltpu.CompilerParams(dimension_semantics=sems, vmem_limit_bytes=VMEM_LIMIT)


def _bdot(a, b):
    return jnp.dot(a, b, preferred_element_type=F32)


def _dot_nt(a, b):
    return lax.dot_general(a, b, (((1,), (1,)), ((), ())), preferred_element_type=F32)


def _dot_tn(a, b):
    return lax.dot_general(a, b, (((0,), (0,)), ((), ())), preferred_element_type=F32)


def _silu(x):
    return x * jax.nn.sigmoid(x)


def _mm_kernel(x_ref, w_ref, o_ref):
    o_ref[...] = _bdot(x_ref[...].astype(BF16), w_ref[...]).astype(o_ref.dtype)


def _matmul(x, w, out_dtype, tm, tn):
    t, k = x.shape
    n = w.shape[1]
    tm = min(tm, t)
    return pl.pallas_call(
        _mm_kernel,
        grid=(n // tn, t // tm),
        in_specs=[pl.BlockSpec((tm, k), lambda j, i: (i, 0)),
                  pl.BlockSpec((k, tn), lambda j, i: (0, j))],
        out_specs=pl.BlockSpec((tm, tn), lambda j, i: (i, j)),
        out_shape=jax.ShapeDtypeStruct((t, n), out_dtype),
        compiler_params=_cparams("parallel", "parallel"),
        name="proj",
    )(x, w)


def _layernorm(h, g, b):
    mu = jnp.mean(h, axis=-1, keepdims=True)
    hc = h - mu
    var = jnp.mean(hc * hc, axis=-1, keepdims=True)
    return hc * lax.rsqrt(var + LN_EPS) * g + b


def _proj_ln_kernel(a_ref, w_ref, x_ref, g_ref, b_ref, o_ref):
    y = _bdot(a_ref[...], w_ref[...])
    o_ref[...] = _layernorm(DEEPNORM_ALPHA * x_ref[...] + y, g_ref[...], b_ref[...])


def _proj_ln(a, w, x, g, b, tm=512):
    t, k = a.shape
    d = w.shape[1]
    tm = min(tm, t)
    return pl.pallas_call(
        _proj_ln_kernel,
        grid=(t // tm,),
        in_specs=[pl.BlockSpec((tm, k), lambda i: (i, 0)),
                  pl.BlockSpec((k, d), lambda i: (0, 0)),
                  pl.BlockSpec((tm, d), lambda i: (i, 0)),
                  pl.BlockSpec((1, d), lambda i: (0, 0)),
                  pl.BlockSpec((1, d), lambda i: (0, 0))],
        out_specs=pl.BlockSpec((tm, d), lambda i: (i, 0)),
        out_shape=jax.ShapeDtypeStruct((t, d), F32),
        compiler_params=_cparams("parallel"),
        name="proj_ln",
    )(a, w, x, g.reshape(1, d), b.reshape(1, d))


CONV_PAD_ROWS = 8
CONV_ROWS = 512


def _conv_kernel(x_ref, w_ref, o_ref, pad_ref):
    c = pl.program_id(1)
    seq = x_ref.shape[1]
    zeros = jnp.zeros((CONV_PAD_ROWS, LANES), F32)
    pad_ref[0:CONV_PAD_ROWS, :] = zeros
    pad_ref[seq + CONV_PAD_ROWS:seq + 2 * CONV_PAD_ROWS, :] = zeros
    pad_ref[CONV_PAD_ROWS:seq + CONV_PAD_ROWS, :] = x_ref[0].astype(F32)
    w = w_ref[...]
    is_q = c < DN_NK
    is_v = c >= 2 * DN_NK
    rows = min(CONV_ROWS, seq)
    half = DN_CONV_W // 2
    for r in range(0, seq, rows):
        acc = None
        for j in range(DN_CONV_W):
            lo = CONV_PAD_ROWS + r + j - half
            term = w[j:j + 1, :] * pad_ref[lo:lo + rows, :]
            acc = term if acc is None else acc + term
        y = _silu(acc)
        nrm = lax.rsqrt(jnp.sum(y * y, axis=-1, keepdims=True) + RMS_EPS)
        nrm = nrm * jnp.where(is_q, DN_DK ** -0.5, 1.0)
        scale = jnp.where(is_v, jnp.ones_like(nrm), nrm)
        o_ref[0, r:r + rows, :] = (y * scale).astype(o_ref.dtype)


def _dn_conv(qkvz, conv_w):
    b, seq, _ = qkvz.shape
    n_tiles = DN_CONV_CH // LANES
    return pl.pallas_call(
        _conv_kernel,
        grid=(b, n_tiles),
        in_specs=[pl.BlockSpec((1, seq, LANES), lambda i, c: (i, 0, c)),
                  pl.BlockSpec((DN_CONV_W, LANES), lambda i, c: (0, c))],
        out_specs=pl.BlockSpec((1, seq, LANES), lambda i, c: (i, 0, c)),
        out_shape=jax.ShapeDtypeStruct((b, seq, DN_CONV_CH), BF16),
        scratch_shapes=[pltpu.VMEM((seq + 2 * CONV_PAD_ROWS, LANES), F32)],
        compiler_params=_cparams("parallel", "parallel"),
        name="dn_conv",
    )(qkvz, conv_w)


def _tri_inverse(a, eye):
    c = a.shape[0]
    x = -a
    q = eye + x
    xb = x.astype(BF16)
    x = _bdot(xb, xb)
    for _ in range(4):
        xb = x.astype(BF16)
        y = _bdot(jnp.concatenate([q.astype(BF16), xb], axis=0), xb)
        q = q + y[:c]
        x = y[c:]
    return q + _bdot(q.astype(BF16), x.astype(BF16))


def _delta_kernel(q_ref, k_ref, v_ref, z_ref, ba_ref, alog_ref, dtb_ref, nw_ref, o_ref,
                  bg_ref, s_ref, acc_ref):
    seq = q_ref.shape[1]
    c = DN_CHUNK
    n_chunks = seq // c
    rows = min(512, seq)

    alog = alog_ref[0]
    dtb = dtb_ref[0]
    lane8 = lax.broadcasted_iota(jnp.int32, (1, 8), 1)
    for r in range(0, seq, rows):
        raw = ba_ref[0, 0, r:r + rows, :]
        beta = jax.nn.sigmoid(raw)
        xx = raw + dtb
        softplus = jnp.maximum(xx, 0.0) + jnp.log(1.0 + jnp.exp(-jnp.abs(xx)))
        g = -jnp.exp(alog) * softplus
        bg_ref[r:r + rows, :] = jnp.where(lane8 < 4, beta, g)
    s_ref[...] = jnp.zeros(s_ref.shape, F32)
    acc_ref[...] = jnp.zeros(acc_ref.shape, F32)

    ii = lax.broadcasted_iota(jnp.int32, (c, c), 0)
    jj = lax.broadcasted_iota(jnp.int32, (c, c), 1)
    eye_m = ii == jj
    eye_f = jnp.where(eye_m, 1.0, 0.0).astype(F32)
    neg_inf = jnp.float32(-jnp.inf)

    def step(i, carry):
        for d in range(2):
            ci = i if d == 0 else n_chunks - 1 - i
            r0 = pl.multiple_of(ci * c, c)
            incl = (ii >= jj) if d == 0 else (ii <= jj)
            strict = (ii > jj) if d == 0 else (ii < jj)
            cum_m = (ii <= jj) if d == 0 else (ii >= jj)
            qc = q_ref[0, pl.ds(r0, c), :]
            kc = k_ref[0, pl.ds(r0, c), :]
            kq = _dot_nt(jnp.concatenate([kc, qc], axis=0), kc)
            kk = kq[:c]
            qk = kq[c:]
            kf = kc.astype(F32)
            qf = qc.astype(F32)
            bg = bg_ref[pl.ds(r0, c), :]
            for h in range(2):
                ch = 2 * d + h
                bcol = bg[:, ch:ch + 1]
                gcol = bg[:, 4 + ch:5 + ch]
                gb = jnp.broadcast_to(gcol, (c, c))
                gc_row = jnp.sum(jnp.where(cum_m, gb, 0.0), axis=0, keepdims=True)
                gc_col = jnp.sum(jnp.where(eye_m, gc_row, 0.0), axis=1, keepdims=True)
                dec = jnp.exp(jnp.where(incl, gc_col - gc_row, neg_inf))
                a = jnp.where(strict, kk * bcol * dec, 0.0)
                t_inv = _tri_inverse(a, eye_f)
                g_last = gc_row[:, c - 1:c] if d == 0 else gc_row[:, 0:1]
                e1 = jnp.exp(gc_col)
                e2 = jnp.exp(g_last - gc_col)
                eg = jnp.exp(g_last)
                vf = v_ref[0, pl.ds(r0, c), h * DN_DV:(h + 1) * DN_DV].astype(F32)
                rhs = jnp.concatenate([vf * bcol, kf * (bcol * e1)], axis=1).astype(BF16)
                uw = _bdot(t_inv.astype(BF16), rhs)
                u = uw[:, :DN_DV]
                w = uw[:, DN_DV:]
                s = s_ref[ch]
                wq = _bdot(jnp.concatenate([w.astype(BF16), (qf * e1).astype(BF16)], axis=0),
                           s.astype(BF16))
                v_new = (u - wq[:c]).astype(BF16)
                o = wq[c:] + _bdot((qk * dec).astype(BF16), v_new)
                s_ref[ch] = s * eg + _dot_tn((kf * e2).astype(BF16), v_new)
                acc_ref[pl.ds(r0, c), h * DN_DV:(h + 1) * DN_DV] += o
        return carry

    lax.fori_loop(0, n_chunks, step, 0)

    nw = nw_ref[...]
    for r in range(0, seq, rows):
        for h in range(2):
            sl = slice(h * DN_DV, (h + 1) * DN_DV)
            o = acc_ref[r:r + rows, sl]
            ms = jnp.mean(o * o, axis=-1, keepdims=True)
            z = z_ref[0, r:r + rows, sl].astype(F32)
            o_ref[0, r:r + rows, sl] = (o * lax.rsqrt(ms + RMS_EPS) * nw * _silu(z)).astype(o_ref.dtype)


def _dn_delta(qkv_n, qkvz, ba_p, alog_p, dtb_p, norm_w):
    b, seq, _ = qkv_n.shape
    pair_w = 2 * DN_DV
    k_tile0 = DN_K_DIM // LANES
    v_tile0 = 2 * DN_K_DIM // pair_w
    z_tile0 = DN_CONV_CH // pair_w
    return pl.pallas_call(
        _delta_kernel,
        grid=(b, DN_NK),
        in_specs=[pl.BlockSpec((1, seq, LANES), lambda i, p: (i, 0, p)),
                  pl.BlockSpec((1, seq, LANES), lambda i, p: (i, 0, k_tile0 + p)),
                  pl.BlockSpec((1, seq, pair_w), lambda i, p: (i, 0, v_tile0 + p)),
                  pl.BlockSpec((1, seq, pair_w), lambda i, p: (i, 0, z_tile0 + p)),
                  pl.BlockSpec((1, 1, seq, 8), lambda i, p: (i, p, 0, 0)),
                  pl.BlockSpec((1, 1, 8), lambda i, p: (p, 0, 0)),
                  pl.BlockSpec((1, 1, 8), lambda i, p: (p, 0, 0)),
                  pl.BlockSpec((1, DN_DV), lambda i, p: (0, 0))],
        out_specs=pl.BlockSpec((1, seq, pair_w), lambda i, p: (i, 0, p)),
        out_shape=jax.ShapeDtypeStruct((b, seq, DN_V_DIM), BF16),
        scratch_shapes=[pltpu.VMEM((seq, 8), F32),
                        pltpu.VMEM((4, DN_DK, DN_DV), F32),
                        pltpu.VMEM((seq, pair_w), F32)],
        compiler_params=_cparams("parallel", "parallel"),
        name="dn_delta",
    )(qkv_n, qkv_n, qkv_n, qkvz, ba_p, alog_p, dtb_p, norm_w.reshape(1, DN_DV))


def _pair_cols(t):
    t = t.astype(F32).reshape(2, DN_NK, 2).transpose(1, 0, 2).reshape(DN_NK, 4)
    return jnp.concatenate([jnp.zeros_like(t), t], axis=1).reshape(DN_NK, 1, 8)


def _deltanet_layer(xf, bsz, seq, w_in, conv_w, a_log, dt_bias, norm_w, w_out, ln_g, ln_b):
    t = xf.shape[0]
    n_qkvz = DN_CONV_CH + DN_V_DIM
    w_bf = w_in.astype(BF16)
    qkvz = _matmul(xf, w_bf[:, :n_qkvz], BF16, tm=1024, tn=1536)
    ba = _matmul(xf, w_bf[:, n_qkvz:], F32, tm=1024, tn=4 * DN_NV)
    ba_p = ba.reshape(bsz, seq, 2, 2, DN_NK, 2).transpose(0, 4, 1, 2, 3, 5).reshape(bsz, DN_NK, seq, 8)
    qkvz = qkvz.reshape(bsz, seq, n_qkvz)
    qkv_n = _dn_conv(qkvz, conv_w)
    o = _dn_delta(qkv_n, qkvz, ba_p, _pair_cols(a_log), _pair_cols(dt_bias), norm_w)
    return _proj_ln(o.reshape(t, DN_V_DIM), w_out.astype(BF16), xf, ln_g, ln_b)


MLA_QK = MLA_NOPE + MLA_ROPE
MLA_IN_EXT = MLA_QR + MLA_KVR + 2 * LANES
MLA_Q_EXT = MLA_NOPE + 2 * MLA_ROPE


def _rope_kernel(pos_ref, invf_ref, sign_ref, cs_ref, sn_ref):
    ang = invf_ref[...] * pos_ref[0]
    cs_ref[0] = jnp.cos(ang)
    sn_ref[0] = jnp.sin(ang) * sign_ref[...]


def _rope_tables(positions):
    b, seq = positions.shape
    half = MLA_ROPE // 2
    inv_freq = 1.0 / (MLA_ROPE_THETA ** (jnp.arange(0, MLA_ROPE, 2, dtype=F32) / MLA_ROPE))
    invf = jnp.concatenate([inv_freq, inv_freq]).reshape(MLA_ROPE, 1)
    sign = jnp.concatenate([-jnp.ones((half,), F32), jnp.ones((half,), F32)]).reshape(MLA_ROPE, 1)
    out = jax.ShapeDtypeStruct((b, MLA_ROPE, seq), F32)
    return pl.pallas_call(
        _rope_kernel,
        grid=(b,),
        in_specs=[pl.BlockSpec((1, 1, seq), lambda i: (i, 0, 0)),
                  pl.BlockSpec((MLA_ROPE, 1), lambda i: (0, 0)),
                  pl.BlockSpec((MLA_ROPE, 1), lambda i: (0, 0))],
        out_specs=[pl.BlockSpec((1, MLA_ROPE, seq), lambda i: (i, 0, 0)),
                   pl.BlockSpec((1, MLA_ROPE, seq), lambda i: (i, 0, 0))],
        out_shape=[out, out],
        compiler_params=_cparams("parallel"),
        name="rope_tables",
    )(positions.astype(F32).reshape(b, 1, seq), invf, sign)


def _rms(x, w):
    return x * lax.rsqrt(jnp.mean(x * x, axis=-1, keepdims=True) + RMS_EPS) * w


def _mla_in_kernel(x_ref, w_ref, qn_ref, kvn_ref, cs_ref, sn_ref, cq_ref, ckv_ref, kr_ref):
    h = _bdot(x_ref[...].astype(BF16), w_ref[...])
    cq_ref[...] = _rms(h[:, :MLA_QR], qn_ref[...]).astype(cq_ref.dtype)
    ckv_ref[...] = _rms(h[:, MLA_QR:MLA_QR + MLA_KVR], kvn_ref[...]).astype(ckv_ref.dtype)
    r0 = MLA_QR + MLA_KVR
    kr = h[:, r0:r0 + MLA_ROPE] * cs_ref[...] + h[:, r0 + LANES:r0 + LANES + MLA_ROPE] * sn_ref[...]
    kr_ref[...] = kr.astype(kr_ref.dtype)


def _mla_in(xf, w_ext, q_norm, kv_norm, cs, sn, tm=512):
    t, d = xf.shape
    tm = min(tm, t)
    row = lambda i: (i, 0)
    fix = lambda i: (0, 0)
    return pl.pallas_call(
        _mla_in_kernel,
        grid=(t // tm,),
        in_specs=[pl.BlockSpec((tm, d), row),
                  pl.BlockSpec((d, MLA_IN_EXT), fix),
                  pl.BlockSpec((1, MLA_QR), fix),
                  pl.BlockSpec((1, MLA_KVR), fix),
                  pl.BlockSpec((tm, MLA_ROPE), row),
                  pl.BlockSpec((tm, MLA_ROPE), row)],
        out_specs=[pl.BlockSpec((tm, MLA_QR), row),
                   pl.BlockSpec((tm, MLA_KVR), row),
                   pl.BlockSpec((tm, MLA_ROPE), row)],
        out_shape=[jax.ShapeDtypeStruct((t, MLA_QR), BF16),
                   jax.ShapeDtypeStruct((t, MLA_KVR), BF16),
                   jax.ShapeDtypeStruct((t, MLA_ROPE), BF16)],
        compiler_params=_cparams("parallel"),
        name="mla_in",
    )(xf, w_ext, q_norm.reshape(1, MLA_QR), kv_norm.reshape(1, MLA_KVR), cs, sn)


def _mla_q_kernel(cq_ref, w_ref, cs_ref, sn_ref, q_ref):
    acc = _dot_nt(w_ref[...], cq_ref[...])
    cs = cs_ref[0]
    sn = sn_ref[0]
    scale = MLA_QK ** -0.5
    for h in range(MLA_H):
        a0 = h * MLA_Q_EXT
        o0 = h * MLA_QK
        q_ref[0, o0:o0 + MLA_NOPE, :] = (acc[a0:a0 + MLA_NOPE] * scale).astype(q_ref.dtype)
        x = acc[a0 + MLA_NOPE:a0 + MLA_NOPE + MLA_ROPE]
        xs = acc[a0 + MLA_NOPE + MLA_ROPE:a0 + MLA_Q_EXT]
        q_ref[0, o0 + MLA_NOPE:o0 + MLA_QK, :] = ((x * cs + xs * sn) * scale).astype(q_ref.dtype)


def _mla_q(cq, wq_t, cs_t, sn_t, bsz, seq, tl=512):
    tl = min(tl, seq)
    nl = seq // tl
    return pl.pallas_call(
        _mla_q_kernel,
        grid=(bsz, nl),
        in_specs=[pl.BlockSpec((tl, MLA_QR), lambda b, i: (b * nl + i, 0)),
                  pl.BlockSpec((MLA_H * MLA_Q_EXT, MLA_QR), lambda b, i: (0, 0)),
                  pl.BlockSpec((1, MLA_ROPE, tl), lambda b, i: (b, 0, i)),
                  pl.BlockSpec((1, MLA_ROPE, tl), lambda b, i: (b, 0, i))],
        out_specs=pl.BlockSpec((1, MLA_H * MLA_QK, tl), lambda b, i: (b, 0, i)),
        out_shape=jax.ShapeDtypeStruct((bsz, MLA_H * MLA_QK, seq), BF16),
        compiler_params=_cparams("parallel", "parallel"),
        name="mla_q",
    )(cq, wq_t, cs_t, sn_t)


def _mla_kv_kernel(ckv_ref, kr_ref, wk_ref, wv_ref, k_ref, v_ref):
    ckv = ckv_ref[...]
    kn = _bdot(ckv, wk_ref[...])
    kr = kr_ref[...]
    for h in range(MLA_H):
        k_ref[0, h] = jnp.concatenate(
            [kn[:, h * MLA_NOPE:(h + 1) * MLA_NOPE].astype(k_ref.dtype), kr], axis=1)
    v_ref[0] = _dot_nt(wv_ref[...], ckv).astype(v_ref.dtype)


def _mla_kv(ckv, kr, wk, wv_t, bsz, seq, tl=512):
    tl = min(tl, seq)
    nl = seq // tl
    return pl.pallas_call(
        _mla_kv_kernel,
        grid=(bsz, nl),
        in_specs=[pl.BlockSpec((tl, MLA_KVR), lambda b, i: (b * nl + i, 0)),
                  pl.BlockSpec((tl, MLA_ROPE), lambda b, i: (b * nl + i, 0)),
                  pl.BlockSpec((MLA_KVR, MLA_H * MLA_NOPE), lambda b, i: (0, 0)),
                  pl.BlockSpec((MLA_H * MLA_V, MLA_KVR), lambda b, i: (0, 0))],
        out_specs=[pl.BlockSpec((1, MLA_H, tl, MLA_QK), lambda b, i: (b, 0, i, 0)),
                   pl.BlockSpec((1, MLA_H * MLA_V, tl), lambda b, i: (b, 0, i))],
        out_shape=[jax.ShapeDtypeStruct((bsz, MLA_H, seq, MLA_QK), BF16),
                   jax.ShapeDtypeStruct((bsz, MLA_H * MLA_V, seq), BF16)],
        compiler_params=_cparams("parallel", "parallel"),
        name="mla_kv",
    )(ckv, kr, wk, wv_t)


ATTN_TQ = 512
ATTN_TK = 512


def _attn_kernel(q_ref, k_ref, v_ref, o_ref):
    q_t = q_ref[0]
    tq = q_t.shape[1]
    seq = k_ref.shape[2]
    tk = min(ATTN_TK, seq)

    def body(j, carry):
        m, l, acc = carry
        k0 = pl.multiple_of(j * tk, tk)
        s = _bdot(k_ref[0, 0, pl.ds(k0, tk), :], q_t)
        m_new = jnp.maximum(m, jnp.max(s, axis=0, keepdims=True))
        alpha = jnp.exp(m - m_new)
        p = jnp.exp(s - m_new)
        l = alpha * l + jnp.sum(p, axis=0, keepdims=True)
        acc = alpha * acc + _bdot(v_ref[0, :, pl.ds(k0, tk)], p.astype(BF16))
        return m_new, l, acc

    m0 = jnp.full((1, tq), -jnp.inf, F32)
    l0 = jnp.zeros((1, tq), F32)
    acc0 = jnp.zeros((MLA_V, tq), F32)
    _, l, acc = lax.fori_loop(0, seq // tk, body, (m0, l0, acc0))
    o_ref[0] = (acc / l).T.astype(o_ref.dtype)


def _mla_attn(q_t, k, v_t, bsz, seq):
    tq = min(ATTN_TQ, seq)
    return pl.pallas_call(
        _attn_kernel,
        grid=(bsz, MLA_H, seq // tq),
        in_specs=[pl.BlockSpec((1, MLA_QK, tq), lambda b, h, i: (b, h, i)),
                  pl.BlockSpec((1, 1, seq, MLA_QK), lambda b, h, i: (b, h, 0, 0)),
                  pl.BlockSpec((1, MLA_V, seq), lambda b, h, i: (b, h, 0))],
        out_specs=pl.BlockSpec((1, tq, MLA_V), lambda b, h, i: (b, i, h)),
        out_shape=jax.ShapeDtypeStruct((bsz, seq, MLA_H * MLA_V), BF16),
        compiler_params=_cparams("parallel", "parallel", "parallel"),
        name="mla_attn",
    )(q_t, k, v_t)


def _swap_halves(t):
    half = t.shape[-1] // 2
    return jnp.concatenate([t[..., half:], t[..., :half]], axis=-1)


def _mla_layer(xf, bsz, seq, cs, sn, cs_t, sn_t, w_in, q_norm, w_qb, kv_norm, w_kvb, w_out, ln_g, ln_b):
    t, d = xf.shape
    r0 = MLA_QR + MLA_KVR
    w_rope = w_in[:, r0:]
    pad = jnp.zeros((d, LANES - MLA_ROPE), w_in.dtype)
    w_ext = jnp.concatenate([w_in, pad, _swap_halves(w_rope), pad], axis=1).astype(BF16)
    wq = w_qb.reshape(MLA_QR, MLA_H, MLA_QK)
    wq_rope = wq[:, :, MLA_NOPE:]
    wq_t = jnp.concatenate([wq, _swap_halves(wq_rope)], axis=-1).reshape(MLA_QR, MLA_H * MLA_Q_EXT).T.astype(BF16)
    wkv = w_kvb.reshape(MLA_KVR, MLA_H, MLA_NOPE + MLA_V)
    wk = wkv[:, :, :MLA_NOPE].reshape(MLA_KVR, MLA_H * MLA_NOPE).astype(BF16)
    wv_t = wkv[:, :, MLA_NOPE:].reshape(MLA_KVR, MLA_H * MLA_V).T.astype(BF16)

    cq, ckv, kr = _mla_in(xf, w_ext, q_norm, kv_norm, cs, sn)
    q_t = _mla_q(cq, wq_t, cs_t, sn_t, bsz, seq)
    k, v_t = _mla_kv(ckv, kr, wk, wv_t, bsz, seq)
    o = _mla_attn(q_t, k, v_t, bsz, seq)
    return _proj_ln(o.reshape(t, MLA_H * MLA_V), w_out.astype(BF16), xf, ln_g, ln_b)


ROUTER_TILE = LANES // N_EXPERTS


def _lane_roll(x, shift):
    return pltpu.roll(x, shift % LANES, 1)


def _ind(mask):
    return jnp.where(mask, 1.0, 0.0)


def _route(logits, bias):
    scores = jax.nn.sigmoid(logits)
    sel = scores + bias
    lane = lax.broadcasted_iota(jnp.int32, sel.shape, 1)
    e_in_g = jnp.bitwise_and(lane, EXPERTS_PER_GROUP - 1)
    g_idx = jnp.right_shift(jnp.bitwise_and(lane, N_EXPERTS - 1), EXPERTS_PER_GROUP.bit_length() - 1)

    def group_nbr(x, k):
        return jnp.where(e_in_g + k < EXPERTS_PER_GROUP,
                         _lane_roll(x, -k), _lane_roll(x, EXPERTS_PER_GROUP - k))

    rank = jnp.zeros(sel.shape, F32)
    for k in range(1, EXPERTS_PER_GROUP):
        other = group_nbr(sel, k)
        other_first = e_in_g + k >= EXPERTS_PER_GROUP
        rank = rank + jnp.where(other_first, _ind(other >= sel), _ind(other > sel))
    top2 = rank < 2.0
    t = jnp.where(top2, sel, 0.0)
    gs = t
    for k in range(1, EXPERTS_PER_GROUP):
        gs = gs + group_nbr(t, k)
    lost = jnp.zeros(sel.shape, F32)
    for k in range(1, N_GROUPS):
        other = _lane_roll(gs, -k * EXPERTS_PER_GROUP)
        other_first = g_idx + k >= N_GROUPS
        lost = lost + jnp.where(other_first, _ind(other >= gs), _ind(other > gs))
    w = jnp.where(top2, jnp.where(lost < 0.5, scores, 0.0), 0.0)
    tot = w
    for sh in (8, 4, 2, 1):
        tot = tot + _lane_roll(tot, sh)
    return w / tot


def _moe_kernel(x_ref, rw_ref, rb_ref, wg_ref, wu_ref, wd_ref, g_ref, b_ref, o_ref,
                xb_ref, gate_ref, acc_ref):
    j = pl.program_id(1)

    @pl.when(j == 0)
    def _():
        x = x_ref[...]
        xb_ref[...] = x.astype(BF16)
        logits = jnp.dot(x, rw_ref[...], precision=lax.Precision.HIGHEST, preferred_element_type=F32)
        gates = _route(logits, rb_ref[...])
        for g in range(N_GROUPS):
            gate_ref[g] = _lane_roll(gates, -g * EXPERTS_PER_GROUP)
        acc_ref[...] = jnp.zeros(acc_ref.shape, F32)

    xb = xb_ref[...]
    gates = gate_ref[j]
    hid = []
    for e in range(EXPERTS_PER_GROUP):
        hg = _bdot(xb, wg_ref[e])
        hu = _bdot(xb, wu_ref[e])
        hid.append((_silu(hg) * hu * gates[:, e:e + 1]).astype(BF16))
    acc_ref[...] += _bdot(jnp.concatenate(hid, axis=1), wd_ref[...])

    @pl.when(j == N_GROUPS - 1)
    def _():
        o_ref[...] = _layernorm(DEEPNORM_ALPHA * x_ref[...] + acc_ref[...], g_ref[...], b_ref[...])


def _moe_layer(xf, rw_t, rb_t, w_gate, w_up, w_down, ln_g, ln_b, tm=512):
    t, d = xf.shape
    tm = min(tm, t)
    f = D_FF_EXPERT
    gf = EXPERTS_PER_GROUP * f
    row = lambda i, j: (i, 0)
    fix = lambda i, j: (0, 0)
    return pl.pallas_call(
        _moe_kernel,
        grid=(t // tm, N_GROUPS),
        in_specs=[pl.BlockSpec((tm, d), row),
                  pl.BlockSpec((d, LANES), fix),
                  pl.BlockSpec((1, LANES), fix),
                  pl.BlockSpec((EXPERTS_PER_GROUP, d, f), lambda i, j: (j, 0, 0)),
                  pl.BlockSpec((EXPERTS_PER_GROUP, d, f), lambda i, j: (j, 0, 0)),
                  pl.BlockSpec((gf, d), lambda i, j: (j, 0)),
                  pl.BlockSpec((1, d), fix),
                  pl.BlockSpec((1, d), fix)],
        out_specs=pl.BlockSpec((tm, d), row),
        out_shape=jax.ShapeDtypeStruct((t, d), F32),
        scratch_shapes=[pltpu.VMEM((tm, d), BF16),
                        pltpu.VMEM((N_GROUPS, tm, LANES), F32),
                        pltpu.VMEM((tm, d), F32)],
        compiler_params=_cparams("parallel", "arbitrary"),
        name="moe",
    )(xf, rw_t, rb_t, w_gate.astype(BF16), w_up.astype(BF16),
      w_down.astype(BF16).reshape(N_EXPERTS * f, d), ln_g.reshape(1, d), ln_b.reshape(1, d))


def kernel(x, positions, dn_w_in, dn_conv_w, dn_a_log, dn_dt_bias, dn_norm_w, dn_w_out, mla_w_in, mla_q_norm, mla_w_qb, mla_kv_norm, mla_w_kvb, mla_w_out, ln1_g, ln1_b, ln2_g, ln2_b, router_w, router_bias, moe_w_gate, moe_w_up, moe_w_down):
    bsz, seq, d = x.shape
    t = bsz * seq
    xf = x.reshape(t, d)
    cs_t, sn_t = _rope_tables(positions)
    cs = cs_t.transpose(0, 2, 1).reshape(t, MLA_ROPE)
    sn = sn_t.transpose(0, 2, 1).reshape(t, MLA_ROPE)
    rw_t = jnp.tile(router_w.astype(F32), (1, ROUTER_TILE))
    rb_t = jnp.tile(router_bias.astype(F32), (ROUTER_TILE,)).reshape(1, LANES)
    for i in range(DEPTH):
        j = i // 2
        if i % 2 == 0:
            xf = _deltanet_layer(xf, bsz, seq, dn_w_in[j], dn_conv_w[j], dn_a_log[j], dn_dt_bias[j],
                                 dn_norm_w[j], dn_w_out[j], ln1_g[i], ln1_b[i])
        else:
            xf = _mla_layer(xf, bsz, seq, cs, sn, cs_t, sn_t, mla_w_in[j], mla_q_norm[j], mla_w_qb[j],
                            mla_kv_norm[j], mla_w_kvb[j], mla_w_out[j], ln1_g[i], ln1_b[i])
        xf = _moe_layer(xf, rw_t, rb_t, moe_w_gate[i], moe_w_up[i], moe_w_down[i], ln2_g[i], ln2_b[i])
    return xf.reshape(bsz, seq, d)
```

```python
import functools

import jax
import jax.numpy as jnp
from jax import lax
from jax.experimental import pallas as pl
from jax.experimental.pallas import tpu as pltpu

F32 = jnp.float32
BF16 = jnp.bfloat16

DEPTH = 4
DN_NK = 8
DN_NV = 16
DN_DK = 128
DN_DV = 128
DN_K_DIM = DN_NK * DN_DK
DN_V_DIM = DN_NV * DN_DV
DN_CONV_CH = 2 * DN_K_DIM + DN_V_DIM
DN_CONV_W = 5
DN_CHUNK = 64
MLA_H = 8
MLA_NOPE = 128
MLA_ROPE = 64
MLA_V = 128
MLA_QR = 384
MLA_KVR = 256
MLA_ROPE_THETA = 10000.0
N_EXPERTS = 16
N_GROUPS = 4
EXPERTS_PER_GROUP = N_EXPERTS // N_GROUPS
D_FF_EXPERT = 256
DEEPNORM_ALPHA = (2 * DEPTH) ** 0.25
LN_EPS = 1e-5
RMS_EPS = 1e-6

LANES = 128
VMEM_LIMIT = 48 * 1024 * 1024


def _cparams(*sems):
    return pltpu.CompilerParams(dimension_semantics=sems, vmem_limit_bytes=VMEM_LIMIT)


def _bdot(a, b):
    return jnp.dot(a, b, preferred_element_type=F32)


def _dot_nt(a, b):
    return lax.dot_general(a, b, (((1,), (1,)), ((), ())), preferred_element_type=F32)


def _dot_tn(a, b):
    return lax.dot_general(a, b, (((0,), (0,)), ((), ())), preferred_element_type=F32)


def _silu(x):
    return x * jax.nn.sigmoid(x)


def _mm_kernel(x_ref, w_ref, o_ref):
    o_ref[...] = _bdot(x_ref[...].astype(BF16), w_ref[...]).astype(o_ref.dtype)


def _matmul(x, w, out_dtype, tm, tn):
    t, k = x.shape
    n = w.shape[1]
    tm = min(tm, t)
    return pl.pallas_call(
        _mm_kernel,
        grid=(n // tn, t // tm),
        in_specs=[pl.BlockSpec((tm, k), lambda j, i: (i, 0)),
                  pl.BlockSpec((k, tn), lambda j, i: (0, j))],
        out_specs=pl.BlockSpec((tm, tn), lambda j, i: (i, j)),
        out_shape=jax.ShapeDtypeStruct((t, n), out_dtype),
        compiler_params=_cparams("parallel", "parallel"),
        name="proj",
    )(x, w)


def _layernorm(h, g, b):
    mu = jnp.mean(h, axis=-1, keepdims=True)
    hc = h - mu
    var = jnp.mean(hc * hc, axis=-1, keepdims=True)
    return hc * lax.rsqrt(var + LN_EPS) * g + b


def _proj_ln_kernel(a_ref, w_ref, x_ref, g_ref, b_ref, o_ref):
    y = _bdot(a_ref[...], w_ref[...])
    o_ref[...] = _layernorm(DEEPNORM_ALPHA * x_ref[...] + y, g_ref[...], b_ref[...])


def _proj_ln(a, w, x, g, b, tm=512):
    t, k = a.shape
    d = w.shape[1]
    tm = min(tm, t)
    return pl.pallas_call(
        _proj_ln_kernel,
        grid=(t // tm,),
        in_specs=[pl.BlockSpec((tm, k), lambda i: (i, 0)),
                  pl.BlockSpec((k, d), lambda i: (0, 0)),
                  pl.BlockSpec((tm, d), lambda i: (i, 0)),
                  pl.BlockSpec((1, d), lambda i: (0, 0)),
                  pl.BlockSpec((1, d), lambda i: (0, 0))],
        out_specs=pl.BlockSpec((tm, d), lambda i: (i, 0)),
        out_shape=jax.ShapeDtypeStruct((t, d), F32),
        compiler_params=_cparams("parallel"),
        name="proj_ln",
    )(a, w, x, g.reshape(1, d), b.reshape(1, d))


CONV_PAD_ROWS = 8
CONV_ROWS = 512


def _conv_kernel(x_ref, w_ref, o_ref, pad_ref):
    c = pl.program_id(1)
    seq = x_ref.shape[1]
    zeros = jnp.zeros((CONV_PAD_ROWS, LANES), F32)
    pad_ref[0:CONV_PAD_ROWS, :] = zeros
    pad_ref[seq + CONV_PAD_ROWS:seq + 2 * CONV_PAD_ROWS, :] = zeros
    pad_ref[CONV_PAD_ROWS:seq + CONV_PAD_ROWS, :] = x_ref[0].astype(F32)
    w = w_ref[...]
    is_q = c < DN_NK
    is_v = c >= 2 * DN_NK
    rows = min(CONV_ROWS, seq)
    half = DN_CONV_W // 2
    for r in range(0, seq, rows):
        acc = None
        for j in range(DN_CONV_W):
            lo = CONV_PAD_ROWS + r + j - half
            term = w[j:j + 1, :] * pad_ref[lo:lo + rows, :]
            acc = term if acc is None else acc + term
        y = _silu(acc)
        nrm = lax.rsqrt(jnp.sum(y * y, axis=-1, keepdims=True) + RMS_EPS)
        nrm = nrm * jnp.where(is_q, DN_DK ** -0.5, 1.0)
        scale = jnp.where(is_v, jnp.ones_like(nrm), nrm)
        o_ref[0, r:r + rows, :] = (y * scale).astype(o_ref.dtype)


def _dn_conv(qkvz, conv_w):
    b, seq, _ = qkvz.shape
    n_tiles = DN_CONV_CH // LANES
    return pl.pallas_call(
        _conv_kernel,
        grid=(b, n_tiles),
        in_specs=[pl.BlockSpec((1, seq, LANES), lambda i, c: (i, 0, c)),
                  pl.BlockSpec((DN_CONV_W, LANES), lambda i, c: (0, c))],
        out_specs=pl.BlockSpec((1, seq, LANES), lambda i, c: (i, 0, c)),
        out_shape=jax.ShapeDtypeStruct((b, seq, DN_CONV_CH), BF16),
        scratch_shapes=[pltpu.VMEM((seq + 2 * CONV_PAD_ROWS, LANES), F32)],
        compiler_params=_cparams("parallel", "parallel"),
        name="dn_conv",
    )(qkvz, conv_w)


DELTA_GROUP = 4
DELTA_BLK = 2 * DN_CHUNK
DELTA_ROWS = DELTA_GROUP * DN_CHUNK
CHUNK_SHIFT = DN_CHUNK.bit_length() - 1


def _block_diag2(x):
    r = x.shape[0]
    z = jnp.zeros((r, r), x.dtype)
    return jnp.concatenate([jnp.concatenate([x[:, :r], z], axis=1),
                            jnp.concatenate([z, x[:, r:]], axis=1)], axis=0)


def _delta_masks():
    n = DELTA_BLK
    ii = lax.broadcasted_iota(jnp.int32, (n, n), 0)
    jj = lax.broadcasted_iota(jnp.int32, (n, n), 1)
    same = jnp.right_shift(ii, CHUNK_SHIFT) == jnp.right_shift(jj, CHUNK_SHIFT)
    ge = jnp.logical_and(same, ii >= jj)
    gt = jnp.logical_and(same, ii > jj)
    le = jnp.logical_and(same, ii <= jj)
    lt = jnp.logical_and(same, ii < jj)
    eye = ii == jj
    eye_f = jnp.where(eye, 1.0, 0.0).astype(F32)
    top = lax.broadcasted_iota(jnp.int32, (n, 1), 0) < DN_CHUNK
    return dict(eye=eye, eye2=jnp.concatenate([eye_f, eye_f], axis=1), top=top,
                incl=(ge, le), strict=(gt, lt), cum=(le, ge))


def _delta_local(q_ref, k_ref, v_ref, bg_ref, bases, mk):
    c, n, dv = DN_CHUNK, DELTA_BLK, DN_DV
    units = [(d, b) for d in range(2) for b in range(DELTA_GROUP // 2)]
    st = {u: {} for u in units}
    neg_inf = jnp.float32(-jnp.inf)

    def stage_kq():
        for d, b in units:
            s = st[(d, b)]
            rs = pl.ds(pl.multiple_of(bases[d] + b * n, n), n)
            kd = k_ref[0, rs, :]
            qd = q_ref[0, rs, :]
            kq = _dot_nt(jnp.concatenate([kd, qd], axis=0), kd)
            kk = kq[:n]
            qk = kq[n:]
            kf = kd.astype(F32)
            qf = qd.astype(F32)
            bg = bg_ref[rs, :]
            a_list = []
            for h in range(2):
                ch = 2 * d + h
                bcol = bg[:, ch:ch + 1]
                gcol = bg[:, 4 + ch:5 + ch]
                gb = jnp.broadcast_to(gcol, (n, n))
                gc_row = jnp.sum(jnp.where(mk["cum"][d], gb, 0.0), axis=0, keepdims=True)
                gc_col = jnp.sum(jnp.where(mk["eye"], gc_row, 0.0), axis=1, keepdims=True)
                dec = jnp.exp(jnp.where(mk["incl"][d], gc_col - gc_row, neg_inf))
                a_list.append(jnp.where(mk["strict"][d], kk * bcol * dec, 0.0))
                gl = (gc_row[:, c - 1:c], gc_row[:, n - 1:n]) if d == 0 else (gc_row[:, 0:1], gc_row[:, c:c + 1])
                gl_col = jnp.where(mk["top"], gl[0], gl[1])
                e1 = jnp.exp(gc_col)
                e2 = jnp.exp(gl_col - gc_col)
                vf = v_ref[0, rs, h * dv:(h + 1) * dv].astype(F32)
                s[h] = dict(rhs=jnp.concatenate([vf * bcol, kf * (bcol * e1)], axis=1).astype(BF16),
                            qe=(qf * e1).astype(BF16), ke=(kf * e2).astype(BF16),
                            qkd=(qk * dec).astype(BF16), eg=[jnp.exp(gl[0]), jnp.exp(gl[1])])
            s["x"] = -jnp.concatenate(a_list, axis=1)
            s["q"] = mk["eye2"] + s["x"]

    def stage_t_first():
        for u in units:
            s = st[u]
            xb = s["x"].astype(BF16)
            s["x"] = _bdot(xb, _block_diag2(xb))

    def stage_t_mid():
        for u in units:
            s = st[u]
            xb = s["x"].astype(BF16)
            y = _bdot(jnp.concatenate([s["q"].astype(BF16), xb], axis=0), _block_diag2(xb))
            s["q"] = s["q"] + y[:n]
            s["x"] = y[n:]

    def stage_t_last():
        for u in units:
            s = st[u]
            s["q"] = s["q"] + _bdot(s["q"].astype(BF16), _block_diag2(s["x"].astype(BF16)))

    def stage_uw():
        for u in units:
            s = st[u]
            for h in range(2):
                uw = _bdot(s["q"][:, h * n:(h + 1) * n].astype(BF16), s[h]["rhs"])
                s[h]["u"] = uw[:, :dv]
                s[h]["w"] = uw[:, dv:].astype(BF16)

    stages = [stage_kq, stage_t_first] + [stage_t_mid] * (CHUNK_SHIFT - 2) + [stage_t_last, stage_uw]
    return stages, st


def _delta_store_local(st, u_s, w_s, qe_s, ke_s, qkd_s, eg_s):
    for d in range(2):
        for h in range(2):
            ch = 2 * d + h
            blocks = [st[(d, b)][h] for b in range(DELTA_GROUP // 2)]
            for name, ref in (("u", u_s), ("w", w_s), ("qe", qe_s), ("ke", ke_s), ("qkd", qkd_s)):
                ref[ch] = jnp.concatenate([blk[name] for blk in blocks], axis=0)
            for b, blk in enumerate(blocks):
                for half in range(2):
                    j = 2 * b + half
                    eg_s[ch, j:j + 1, :] = jnp.broadcast_to(blk["eg"][half], (1, DN_DV))


def _delta_scan(u_s, w_s, qe_s, ke_s, qkd_s, eg_s, s_ref, acc_ref, bases):
    c = DN_CHUNK
    stages = []
    for jf in range(DELTA_GROUP):
        wq = {}

        def chunk_of(ch, jf=jf):
            return jf if ch < 2 else DELTA_GROUP - 1 - jf

        def stage1(wq=wq, chunk_of=chunk_of):
            for ch in range(4):
                j = chunk_of(ch)
                rl = slice(j * c, (j + 1) * c)
                lhs = jnp.concatenate([w_s[ch, rl, :], qe_s[ch, rl, :]], axis=0)
                wq[ch] = _bdot(lhs, s_ref[ch].astype(BF16))

        def stage2(wq=wq, chunk_of=chunk_of):
            zpad = jnp.zeros((c, DN_DV), BF16)
            for ch in range(4):
                j = chunk_of(ch)
                rl = slice(j * c, (j + 1) * c)
                v_new = (u_s[ch, rl, :] - wq[ch][:c]).astype(BF16)
                v_pad = jnp.concatenate([v_new, zpad] if j % 2 == 0 else [zpad, v_new], axis=0)
                o = wq[ch][c:] + _bdot(qkd_s[ch, rl, :], v_pad)
                s_ref[ch] = s_ref[ch] * eg_s[ch, j:j + 1, :] + _dot_tn(ke_s[ch, rl, :], v_new)
                acc_ref[ch, pl.ds(pl.multiple_of(bases[ch // 2] + j * c, c), c), :] = o

        stages += [stage1, stage2]
    return stages


def _delta_kernel(q_ref, k_ref, v_ref, z_ref, ba_ref, alog_ref, dtb_ref, nw_ref, o_ref,
                  bg_ref, s_ref, acc_ref, u_s, w_s, qe_s, ke_s, qkd_s, eg_s):
    seq = q_ref.shape[1]
    n_groups = seq // DELTA_ROWS
    rows = min(512, seq)

    alog = alog_ref[0]
    dtb = dtb_ref[0]
    lane8 = lax.broadcasted_iota(jnp.int32, (1, 8), 1)
    for r in range(0, seq, rows):
        raw = ba_ref[0, 0, r:r + rows, :]
        beta = jax.nn.sigmoid(raw)
        xx = raw + dtb
        softplus = jnp.maximum(xx, 0.0) + jnp.log(1.0 + jnp.exp(-jnp.abs(xx)))
        g = -jnp.exp(alog) * softplus
        bg_ref[r:r + rows, :] = jnp.where(lane8 < 4, beta, g)
    s_ref[...] = jnp.zeros(s_ref.shape, F32)

    mk = _delta_masks()
    local_scratch = (u_s, w_s, qe_s, ke_s, qkd_s, eg_s)

    def group_bases(t):
        return (pl.multiple_of(t * DELTA_ROWS, DELTA_ROWS),
                pl.multiple_of((n_groups - 1 - t) * DELTA_ROWS, DELTA_ROWS))

    stages, st = _delta_local(q_ref, k_ref, v_ref, bg_ref, (0, (n_groups - 1) * DELTA_ROWS), mk)
    for stage in stages:
        stage()
    _delta_store_local(st, *local_scratch)

    def trip(t, carry):
        nxt = jnp.minimum(t + 1, n_groups - 1)
        l_stages, l_st = _delta_local(q_ref, k_ref, v_ref, bg_ref, group_bases(nxt), mk)
        s_stages = _delta_scan(*local_scratch, s_ref, acc_ref, group_bases(t))
        for a, b in zip(l_stages, s_stages):
            a()
            b()
        _delta_store_local(l_st, *local_scratch)
        return carry

    lax.fori_loop(0, n_groups, trip, 0)

    nw = nw_ref[...]
    for r in range(0, seq, rows):
        for h in range(2):
            sl = slice(h * DN_DV, (h + 1) * DN_DV)
            o = acc_ref[h, r:r + rows, :] + acc_ref[2 + h, r:r + rows, :]
            ms = jnp.mean(o * o, axis=-1, keepdims=True)
            z = z_ref[0, r:r + rows, sl].astype(F32)
            o_ref[0, r:r + rows, sl] = (o * lax.rsqrt(ms + RMS_EPS) * nw * _silu(z)).astype(o_ref.dtype)


def _dn_delta(qkv_n, qkvz, ba_p, alog_p, dtb_p, norm_w):
    b, seq, _ = qkv_n.shape
    assert seq % DELTA_ROWS == 0 and len(_delta_local(None, None, None, None, None, None)[0]) == 2 * DELTA_GROUP
    pair_w = 2 * DN_DV
    k_tile0 = DN_K_DIM // LANES
    v_tile0 = 2 * DN_K_DIM // pair_w
    z_tile0 = DN_CONV_CH // pair_w
    grp = lambda dt: pltpu.VMEM((4, DELTA_ROWS, DN_DV), dt)
    return pl.pallas_call(
        _delta_kernel,
        grid=(b, DN_NK),
        in_specs=[pl.BlockSpec((1, seq, LANES), lambda i, p: (i, 0, p)),
                  pl.BlockSpec((1, seq, LANES), lambda i, p: (i, 0, k_tile0 + p)),
                  pl.BlockSpec((1, seq, pair_w), lambda i, p: (i, 0, v_tile0 + p)),
                  pl.BlockSpec((1, seq, pair_w), lambda i, p: (i, 0, z_tile0 + p)),
                  pl.BlockSpec((1, 1, seq, 8), lambda i, p: (i, p, 0, 0)),
                  pl.BlockSpec((1, 1, 8), lambda i, p: (p, 0, 0)),
                  pl.BlockSpec((1, 1, 8), lambda i, p: (p, 0, 0)),
                  pl.BlockSpec((1, DN_DV), lambda i, p: (0, 0))],
        out_specs=pl.BlockSpec((1, seq, pair_w), lambda i, p: (i, 0, p)),
        out_shape=jax.ShapeDtypeStruct((b, seq, DN_V_DIM), BF16),
        scratch_shapes=[pltpu.VMEM((seq, 8), F32),
                        pltpu.VMEM((4, DN_DK, DN_DV), F32),
                        pltpu.VMEM((4, seq, DN_DV), F32),
                        grp(F32), grp(BF16), grp(BF16), grp(BF16), grp(BF16),
                        pltpu.VMEM((4, 8, DN_DV), F32)],
        compiler_params=_cparams("parallel", "parallel"),
        name="dn_delta",
    )(qkv_n, qkv_n, qkv_n, qkvz, ba_p, alog_p, dtb_p, norm_w.reshape(1, DN_DV))


def _pair_cols(t):
    t = t.astype(F32).reshape(2, DN_NK, 2).transpose(1, 0, 2).reshape(DN_NK, 4)
    return jnp.concatenate([jnp.zeros_like(t), t], axis=1).reshape(DN_NK, 1, 8)


def _deltanet_layer(xf, bsz, seq, w_in, conv_w, a_log, dt_bias, norm_w, w_out, ln_g, ln_b):
    t = xf.shape[0]
    n_qkvz = DN_CONV_CH + DN_V_DIM
    w_bf = w_in.astype(BF16)
    qkvz = _matmul(xf, w_bf[:, :n_qkvz], BF16, tm=1024, tn=1536)
    ba = _matmul(xf, w_bf[:, n_qkvz:], F32, tm=1024, tn=4 * DN_NV)
    ba_p = ba.reshape(bsz, seq, 2, 2, DN_NK, 2).transpose(0, 4, 1, 2, 3, 5).reshape(bsz, DN_NK, seq, 8)
    qkvz = qkvz.reshape(bsz, seq, n_qkvz)
    qkv_n = _dn_conv(qkvz, conv_w)
    o = _dn_delta(qkv_n, qkvz, ba_p, _pair_cols(a_log), _pair_cols(dt_bias), norm_w)
    return _proj_ln(o.reshape(t, DN_V_DIM), w_out.astype(BF16), xf, ln_g, ln_b)


MLA_QK = MLA_NOPE + MLA_ROPE
MLA_IN_EXT = MLA_QR + MLA_KVR + 2 * LANES
MLA_Q_EXT = MLA_NOPE + 2 * MLA_ROPE
MLA_V_EXT = MLA_V + 16
LOG2E = 1.4426950408889634


def _rope_kernel(pos_ref, invf_ref, sign_ref, cs_ref, sn_ref):
    ang = invf_ref[...] * pos_ref[0]
    cs_ref[0] = jnp.cos(ang)
    sn_ref[0] = jnp.sin(ang) * sign_ref[...]


def _rope_tables(positions):
    b, seq = positions.shape
    half = MLA_ROPE // 2
    inv_freq = 1.0 / (MLA_ROPE_THETA ** (jnp.arange(0, MLA_ROPE, 2, dtype=F32) / MLA_ROPE))
    invf = jnp.concatenate([inv_freq, inv_freq]).reshape(MLA_ROPE, 1)
    sign = jnp.concatenate([-jnp.ones((half,), F32), jnp.ones((half,), F32)]).reshape(MLA_ROPE, 1)
    out = jax.ShapeDtypeStruct((b, MLA_ROPE, seq), F32)
    return pl.pallas_call(
        _rope_kernel,
        grid=(b,),
        in_specs=[pl.BlockSpec((1, 1, seq), lambda i: (i, 0, 0)),
                  pl.BlockSpec((MLA_ROPE, 1), lambda i: (0, 0)),
                  pl.BlockSpec((MLA_ROPE, 1), lambda i: (0, 0))],
        out_specs=[pl.BlockSpec((1, MLA_ROPE, seq), lambda i: (i, 0, 0)),
                   pl.BlockSpec((1, MLA_ROPE, seq), lambda i: (i, 0, 0))],
        out_shape=[out, out],
        compiler_params=_cparams("parallel"),
        name="rope_tables",
    )(positions.astype(F32).reshape(b, 1, seq), invf, sign)


def _rms(x, w):
    return x * lax.rsqrt(jnp.mean(x * x, axis=-1, keepdims=True) + RMS_EPS) * w


def _mla_in_kernel(x_ref, w_ref, qn_ref, kvn_ref, cs_ref, sn_ref, cq_ref, ckv_ref, kr_ref):
    h = _bdot(x_ref[...].astype(BF16), w_ref[...])
    cq_ref[...] = _rms(h[:, :MLA_QR], qn_ref[...]).astype(cq_ref.dtype)
    ckv_ref[...] = _rms(h[:, MLA_QR:MLA_QR + MLA_KVR], kvn_ref[...]).astype(ckv_ref.dtype)
    r0 = MLA_QR + MLA_KVR
    kr = h[:, r0:r0 + MLA_ROPE] * cs_ref[...] + h[:, r0 + LANES:r0 + LANES + MLA_ROPE] * sn_ref[...]
    kr_ref[...] = kr.astype(kr_ref.dtype)


def _mla_in(xf, w_ext, q_norm, kv_norm, cs, sn, tm=512):
    t, d = xf.shape
    tm = min(tm, t)
    row = lambda i: (i, 0)
    fix = lambda i: (0, 0)
    return pl.pallas_call(
        _mla_in_kernel,
        grid=(t // tm,),
        in_specs=[pl.BlockSpec((tm, d), row),
                  pl.BlockSpec((d, MLA_IN_EXT), fix),
                  pl.BlockSpec((1, MLA_QR), fix),
                  pl.BlockSpec((1, MLA_KVR), fix),
                  pl.BlockSpec((tm, MLA_ROPE), row),
                  pl.BlockSpec((tm, MLA_ROPE), row)],
        out_specs=[pl.BlockSpec((tm, MLA_QR), row),
                   pl.BlockSpec((tm, MLA_KVR), row),
                   pl.BlockSpec((tm, MLA_ROPE), row)],
        out_shape=[jax.ShapeDtypeStruct((t, MLA_QR), BF16),
                   jax.ShapeDtypeStruct((t, MLA_KVR), BF16),
                   jax.ShapeDtypeStruct((t, MLA_ROPE), BF16)],
        compiler_params=_cparams("parallel"),
        name="mla_in",
    )(xf, w_ext, q_norm.reshape(1, MLA_QR), kv_norm.reshape(1, MLA_KVR), cs, sn)


def _mla_q_kernel(cq_ref, w_ref, cs_ref, sn_ref, q_ref):
    acc = _dot_nt(w_ref[...], cq_ref[...])
    cs = cs_ref[0]
    sn = sn_ref[0]
    scale = MLA_QK ** -0.5 * LOG2E
    for h in range(MLA_H):
        a0 = h * MLA_Q_EXT
        o0 = h * MLA_QK
        q_ref[0, o0:o0 + MLA_NOPE, :] = (acc[a0:a0 + MLA_NOPE] * scale).astype(q_ref.dtype)
        x = acc[a0 + MLA_NOPE:a0 + MLA_NOPE + MLA_ROPE]
        xs = acc[a0 + MLA_NOPE + MLA_ROPE:a0 + MLA_Q_EXT]
        q_ref[0, o0 + MLA_NOPE:o0 + MLA_QK, :] = ((x * cs + xs * sn) * scale).astype(q_ref.dtype)


def _mla_q(cq, wq_t, cs_t, sn_t, bsz, seq, tl=512):
    tl = min(tl, seq)
    nl = seq // tl
    return pl.pallas_call(
        _mla_q_kernel,
        grid=(bsz, nl),
        in_specs=[pl.BlockSpec((tl, MLA_QR), lambda b, i: (b * nl + i, 0)),
                  pl.BlockSpec((MLA_H * MLA_Q_EXT, MLA_QR), lambda b, i: (0, 0)),
                  pl.BlockSpec((1, MLA_ROPE, tl), lambda b, i: (b, 0, i)),
                  pl.BlockSpec((1, MLA_ROPE, tl), lambda b, i: (b, 0, i))],
        out_specs=pl.BlockSpec((1, MLA_H * MLA_QK, tl), lambda b, i: (b, 0, i)),
        out_shape=jax.ShapeDtypeStruct((bsz, MLA_H * MLA_QK, seq), BF16),
        compiler_params=_cparams("parallel", "parallel"),
        name="mla_q",
    )(cq, wq_t, cs_t, sn_t)


def _mla_kv_kernel(ckv_ref, kr_ref, wk_ref, wv_ref, k_ref, v_ref):
    ckv = ckv_ref[...]
    kn = _bdot(ckv, wk_ref[...])
    kr = kr_ref[...]
    for h in range(MLA_H):
        k_ref[0, h] = jnp.concatenate(
            [kn[:, h * MLA_NOPE:(h + 1) * MLA_NOPE].astype(k_ref.dtype), kr], axis=1)
    v_t = _dot_nt(wv_ref[...], ckv).astype(v_ref.dtype)
    ones = jnp.ones((MLA_V_EXT - MLA_V, v_t.shape[1]), v_ref.dtype)
    for h in range(MLA_H):
        v_ref[0, h * MLA_V_EXT:h * MLA_V_EXT + MLA_V, :] = v_t[h * MLA_V:(h + 1) * MLA_V]
        v_ref[0, h * MLA_V_EXT + MLA_V:(h + 1) * MLA_V_EXT, :] = ones


def _mla_kv(ckv, kr, wk, wv_t, bsz, seq, tl=512):
    tl = min(tl, seq)
    nl = seq // tl
    return pl.pallas_call(
        _mla_kv_kernel,
        grid=(bsz, nl),
        in_specs=[pl.BlockSpec((tl, MLA_KVR), lambda b, i: (b * nl + i, 0)),
                  pl.BlockSpec((tl, MLA_ROPE), lambda b, i: (b * nl + i, 0)),
                  pl.BlockSpec((MLA_KVR, MLA_H * MLA_NOPE), lambda b, i: (0, 0)),
                  pl.BlockSpec((MLA_H * MLA_V, MLA_KVR), lambda b, i: (0, 0))],
        out_specs=[pl.BlockSpec((1, MLA_H, tl, MLA_QK), lambda b, i: (b, 0, i, 0)),
                   pl.BlockSpec((1, MLA_H * MLA_V_EXT, tl), lambda b, i: (b, 0, i))],
        out_shape=[jax.ShapeDtypeStruct((bsz, MLA_H, seq, MLA_QK), BF16),
                   jax.ShapeDtypeStruct((bsz, MLA_H * MLA_V_EXT, seq), BF16)],
        compiler_params=_cparams("parallel", "parallel"),
        name="mla_kv",
    )(ckv, kr, wk, wv_t)


ATTN_TQ = 512
ATTN_TK = 512


def _attn_kernel(q_ref, k_ref, v_ref, o_ref):
    q_t = q_ref[0]
    tq = q_t.shape[1]
    seq = k_ref.shape[2]
    tk = min(ATTN_TK, seq)
    m = jnp.full((1, tq), -jnp.inf, F32)
    acc = jnp.zeros((MLA_V_EXT, tq), F32)
    n_k = seq // tk
    scores = lambda j: _bdot(k_ref[0, 0, j * tk:(j + 1) * tk, :], q_t)
    s_next = scores(0)
    for j in range(n_k):
        s = s_next
        if j + 1 < n_k:
            s_next = scores(j + 1)
        m_new = jnp.maximum(m, jnp.max(s, axis=0, keepdims=True))
        p = jnp.exp2(s - m_new).astype(BF16)
        acc = jnp.exp2(m - m_new) * acc + _bdot(v_ref[0, :, j * tk:(j + 1) * tk], p)
        m = m_new
    o_ref[0] = (acc[:MLA_V] / acc[MLA_V:MLA_V + 1]).T.astype(o_ref.dtype)


def _mla_attn(q_t, k, v_t, bsz, seq):
    tq = min(ATTN_TQ, seq)
    return pl.pallas_call(
        _attn_kernel,
        grid=(bsz, MLA_H, seq // tq),
        in_specs=[pl.BlockSpec((1, MLA_QK, tq), lambda b, h, i: (b, h, i)),
                  pl.BlockSpec((1, 1, seq, MLA_QK), lambda b, h, i: (b, h, 0, 0)),
                  pl.BlockSpec((1, MLA_V_EXT, seq), lambda b, h, i: (b, h, 0))],
        out_specs=pl.BlockSpec((1, tq, MLA_V), lambda b, h, i: (b, i, h)),
        out_shape=jax.ShapeDtypeStruct((bsz, seq, MLA_H * MLA_V), BF16),
        compiler_params=_cparams("parallel", "parallel", "parallel"),
        name="mla_attn",
    )(q_t, k, v_t)


def _swap_halves(t):
    half = t.shape[-1] // 2
    return jnp.concatenate([t[..., half:], t[..., :half]], axis=-1)


def _mla_layer(xf, bsz, seq, cs, sn, cs_t, sn_t, w_in, q_norm, w_qb, kv_norm, w_kvb, w_out, ln_g, ln_b):
    t, d = xf.shape
    r0 = MLA_QR + MLA_KVR
    w_rope = w_in[:, r0:]
    pad = jnp.zeros((d, LANES - MLA_ROPE), w_in.dtype)
    w_ext = jnp.concatenate([w_in, pad, _swap_halves(w_rope), pad], axis=1).astype(BF16)
    wq = w_qb.reshape(MLA_QR, MLA_H, MLA_QK)
    wq_rope = wq[:, :, MLA_NOPE:]
    wq_t = jnp.concatenate([wq, _swap_halves(wq_rope)], axis=-1).reshape(MLA_QR, MLA_H * MLA_Q_EXT).T.astype(BF16)
    wkv = w_kvb.reshape(MLA_KVR, MLA_H, MLA_NOPE + MLA_V)
    wk = wkv[:, :, :MLA_NOPE].reshape(MLA_KVR, MLA_H * MLA_NOPE).astype(BF16)
    wv_t = wkv[:, :, MLA_NOPE:].reshape(MLA_KVR, MLA_H * MLA_V).T.astype(BF16)

    cq, ckv, kr = _mla_in(xf, w_ext, q_norm, kv_norm, cs, sn)
    q_t = _mla_q(cq, wq_t, cs_t, sn_t, bsz, seq)
    k, v_t = _mla_kv(ckv, kr, wk, wv_t, bsz, seq)
    o = _mla_attn(q_t, k, v_t, bsz, seq)
    return _proj_ln(o.reshape(t, MLA_H * MLA_V), w_out.astype(BF16), xf, ln_g, ln_b)


ROUTER_TILE = LANES // N_EXPERTS


def _lane_roll(x, shift):
    return pltpu.roll(x, shift % LANES, 1)


def _ind(mask):
    return jnp.where(mask, 1.0, 0.0)


def _route(logits, bias):
    scores = jax.nn.sigmoid(logits)
    sel = scores + bias
    lane = lax.broadcasted_iota(jnp.int32, sel.shape, 1)
    e_in_g = jnp.bitwise_and(lane, EXPERTS_PER_GROUP - 1)
    g_idx = jnp.right_shift(jnp.bitwise_and(lane, N_EXPERTS - 1), EXPERTS_PER_GROUP.bit_length() - 1)

    def group_nbr(x, k):
        return jnp.where(e_in_g + k < EXPERTS_PER_GROUP,
                         _lane_roll(x, -k), _lane_roll(x, EXPERTS_PER_GROUP - k))

    rank = jnp.zeros(sel.shape, F32)
    for k in range(1, EXPERTS_PER_GROUP):
        other = group_nbr(sel, k)
        other_first = e_in_g + k >= EXPERTS_PER_GROUP
        rank = rank + jnp.where(other_first, _ind(other >= sel), _ind(other > sel))
    top2 = rank < 2.0
    t = jnp.where(top2, sel, 0.0)
    gs = t
    for k in range(1, EXPERTS_PER_GROUP):
        gs = gs + group_nbr(t, k)
    lost = jnp.zeros(sel.shape, F32)
    for k in range(1, N_GROUPS):
        other = _lane_roll(gs, -k * EXPERTS_PER_GROUP)
        other_first = g_idx + k >= N_GROUPS
        lost = lost + jnp.where(other_first, _ind(other >= gs), _ind(other > gs))
    w = jnp.where(top2, jnp.where(lost < 0.5, scores, 0.0), 0.0)
    tot = w
    for sh in (8, 4, 2, 1):
        tot = tot + _lane_roll(tot, sh)
    return w / tot


def _moe_kernel(x_ref, rw_ref, rb_ref, wg_ref, wu_ref, wd_ref, g_ref, b_ref, o_ref,
                xb_ref, gate_ref, acc_ref):
    j = pl.program_id(1)

    @pl.when(j == 0)
    def _():
        x = x_ref[...]
        xb_ref[...] = x.astype(BF16)
        logits = jnp.dot(x, rw_ref[...], precision=lax.Precision.HIGHEST, preferred_element_type=F32)
        gates = _route(logits, rb_ref[...])
        for g in range(N_GROUPS):
            gate_ref[g] = _lane_roll(gates, -g * EXPERTS_PER_GROUP)
        acc_ref[...] = jnp.zeros(acc_ref.shape, F32)

    xb = xb_ref[...]
    gates = gate_ref[j]
    hid = []
    for e in range(EXPERTS_PER_GROUP):
        hg = _bdot(xb, wg_ref[e])
        hu = _bdot(xb, wu_ref[e])
        hid.append((_silu(hg) * hu * gates[:, e:e + 1]).astype(BF16))
    acc_ref[...] += _bdot(jnp.concatenate(hid, axis=1), wd_ref[...])

    @pl.when(j == N_GROUPS - 1)
    def _():
        o_ref[...] = _layernorm(DEEPNORM_ALPHA * x_ref[...] + acc_ref[...], g_ref[...], b_ref[...])


def _moe_layer(xf, rw_t, rb_t, w_gate, w_up, w_down, ln_g, ln_b, tm=512):
    t, d = xf.shape
    tm = min(tm, t)
    f = D_FF_EXPERT
    gf = EXPERTS_PER_GROUP * f
    row = lambda i, j: (i, 0)
    fix = lambda i, j: (0, 0)
    return pl.pallas_call(
        _moe_kernel,
        grid=(t // tm, N_GROUPS),
        in_specs=[pl.BlockSpec((tm, d), row),
                  pl.BlockSpec((d, LANES), fix),
                  pl.BlockSpec((1, LANES), fix),
                  pl.BlockSpec((EXPERTS_PER_GROUP, d, f), lambda i, j: (j, 0, 0)),
                  pl.BlockSpec((EXPERTS_PER_GROUP, d, f), lambda i, j: (j, 0, 0)),
                  pl.BlockSpec((gf, d), lambda i, j: (j, 0)),
                  pl.BlockSpec((1, d), fix),
                  pl.BlockSpec((1, d), fix)],
        out_specs=pl.BlockSpec((tm, d), row),
        out_shape=jax.ShapeDtypeStruct((t, d), F32),
        scratch_shapes=[pltpu.VMEM((tm, d), BF16),
                        pltpu.VMEM((N_GROUPS, tm, LANES), F32),
                        pltpu.VMEM((tm, d), F32)],
        compiler_params=_cparams("parallel", "arbitrary"),
        name="moe",
    )(xf, rw_t, rb_t, w_gate.astype(BF16), w_up.astype(BF16),
      w_down.astype(BF16).reshape(N_EXPERTS * f, d), ln_g.reshape(1, d), ln_b.reshape(1, d))


def kernel(x, positions, dn_w_in, dn_conv_w, dn_a_log, dn_dt_bias, dn_norm_w, dn_w_out, mla_w_in, mla_q_norm, mla_w_qb, mla_kv_norm, mla_w_kvb, mla_w_out, ln1_g, ln1_b, ln2_g, ln2_b, router_w, router_bias, moe_w_gate, moe_w_up, moe_w_down):
    bsz, seq, d = x.shape
    t = bsz * seq
    xf = x.reshape(t, d)
    cs_t, sn_t = _rope_tables(positions)
    cs = cs_t.transpose(0, 2, 1).reshape(t, MLA_ROPE)
    sn = sn_t.transpose(0, 2, 1).reshape(t, MLA_ROPE)
    rw_t = jnp.tile(router_w.astype(F32), (1, ROUTER_TILE))
    rb_t = jnp.tile(router_bias.astype(F32), (ROUTER_TILE,)).reshape(1, LANES)
    for i in range(DEPTH):
        j = i // 2
        if i % 2 == 0:
            xf = _deltanet_layer(xf, bsz, seq, dn_w_in[j], dn_conv_w[j], dn_a_log[j], dn_dt_bias[j],
                                 dn_norm_w[j], dn_w_out[j], ln1_g[i], ln1_b[i])
        else:
            xf = _mla_layer(xf, bsz, seq, cs, sn, cs_t, sn_t, mla_w_in[j], mla_q_norm[j], mla_w_qb[j],
                            mla_kv_norm[j], mla_w_kvb[j], mla_w_out[j], ln1_g[i], ln1_b[i])
        xf = _moe_layer(xf, rw_t, rb_t, moe_w_gate[i], moe_w_up[i], moe_w_down[i], ln2_g[i], ln2_b[i])
    return xf.reshape(bsz, seq, d)
```

```python
import functools

import jax
import jax.numpy as jnp
from jax import lax
from jax.experimental import pallas as pl
from jax.experimental.pallas import tpu as pltpu

F32 = jnp.float32
BF16 = jnp.bfloat16

DEPTH = 4
DN_NK = 8
DN_NV = 16
DN_DK = 128
DN_DV = 128
DN_K_DIM = DN_NK * DN_DK
DN_V_DIM = DN_NV * DN_DV
DN_CONV_CH = 2 * DN_K_DIM + DN_V_DIM
DN_CONV_W = 5
DN_CHUNK = 64
MLA_H = 8
MLA_NOPE = 128
MLA_ROPE = 64
MLA_V = 128
MLA_QR = 384
MLA_KVR = 256
MLA_ROPE_THETA = 10000.0
N_EXPERTS = 16
N_GROUPS = 4
EXPERTS_PER_GROUP = N_EXPERTS // N_GROUPS
D_FF_EXPERT = 256
DEEPNORM_ALPHA = (2 * DEPTH) ** 0.25
LN_EPS = 1e-5
RMS_EPS = 1e-6

LANES = 128
VMEM_LIMIT = 48 * 1024 * 1024


def _cparams(*sems, vmem=VMEM_LIMIT):
    return pltpu.CompilerParams(dimension_semantics=sems, vmem_limit_bytes=vmem)


def _bdot(a, b):
    return jnp.dot(a, b, preferred_element_type=F32)


def _dot_nt(a, b):
    return lax.dot_general(a, b, (((1,), (1,)), ((), ())), preferred_element_type=F32)


def _dot_tn(a, b):
    return lax.dot_general(a, b, (((0,), (0,)), ((), ())), preferred_element_type=F32)


def _silu(x):
    return x * jax.nn.sigmoid(x)


def _mm_kernel(x_ref, w_ref, o_ref):
    o_ref[...] = _bdot(x_ref[...].astype(BF16), w_ref[...]).astype(o_ref.dtype)


def _matmul(x, w, out_dtype, tm, tn):
    t, k = x.shape
    n = w.shape[1]
    tm = min(tm, t)
    return pl.pallas_call(
        _mm_kernel,
        grid=(n // tn, t // tm),
        in_specs=[pl.BlockSpec((tm, k), lambda j, i: (i, 0)),
                  pl.BlockSpec((k, tn), lambda j, i: (0, j))],
        out_specs=pl.BlockSpec((tm, tn), lambda j, i: (i, j)),
        out_shape=jax.ShapeDtypeStruct((t, n), out_dtype),
        compiler_params=_cparams("parallel", "parallel"),
        name="proj",
    )(x, w)


def _layernorm(h, g, b):
    mu = jnp.mean(h, axis=-1, keepdims=True)
    hc = h - mu
    var = jnp.mean(hc * hc, axis=-1, keepdims=True)
    return hc * lax.rsqrt(var + LN_EPS) * g + b


def _proj_ln_kernel(a_ref, w_ref, x_ref, g_ref, b_ref, o_ref):
    y = _bdot(a_ref[...], w_ref[...])
    o_ref[...] = _layernorm(DEEPNORM_ALPHA * x_ref[...] + y, g_ref[...], b_ref[...])


def _proj_ln(a, w, x, g, b, tm=512):
    t, k = a.shape
    d = w.shape[1]
    tm = min(tm, t)
    return pl.pallas_call(
        _proj_ln_kernel,
        grid=(t // tm,),
        in_specs=[pl.BlockSpec((tm, k), lambda i: (i, 0)),
                  pl.BlockSpec((k, d), lambda i: (0, 0)),
                  pl.BlockSpec((tm, d), lambda i: (i, 0)),
                  pl.BlockSpec((1, d), lambda i: (0, 0)),
                  pl.BlockSpec((1, d), lambda i: (0, 0))],
        out_specs=pl.BlockSpec((tm, d), lambda i: (i, 0)),
        out_shape=jax.ShapeDtypeStruct((t, d), F32),
        compiler_params=_cparams("parallel"),
        name="proj_ln",
    )(a, w, x, g.reshape(1, d), b.reshape(1, d))


CONV_PAD_ROWS = 8
CONV_ROWS = 512


def _conv_kernel(x_ref, w_ref, o_ref, pad_ref):
    c = pl.program_id(1)
    seq = x_ref.shape[1]
    zeros = jnp.zeros((CONV_PAD_ROWS, LANES), F32)
    pad_ref[0:CONV_PAD_ROWS, :] = zeros
    pad_ref[seq + CONV_PAD_ROWS:seq + 2 * CONV_PAD_ROWS, :] = zeros
    pad_ref[CONV_PAD_ROWS:seq + CONV_PAD_ROWS, :] = x_ref[0].astype(F32)
    w = w_ref[...]
    is_q = c < DN_NK
    is_v = c >= 2 * DN_NK
    rows = min(CONV_ROWS, seq)
    half = DN_CONV_W // 2
    for r in range(0, seq, rows):
        acc = None
        for j in range(DN_CONV_W):
            lo = CONV_PAD_ROWS + r + j - half
            term = w[j:j + 1, :] * pad_ref[lo:lo + rows, :]
            acc = term if acc is None else acc + term
        y = _silu(acc)
        nrm = lax.rsqrt(jnp.sum(y * y, axis=-1, keepdims=True) + RMS_EPS)
        nrm = nrm * jnp.where(is_q, DN_DK ** -0.5, 1.0)
        scale = jnp.where(is_v, jnp.ones_like(nrm), nrm)
        o_ref[0, r:r + rows, :] = (y * scale).astype(o_ref.dtype)


def _dn_conv(qkvz, conv_w):
    b, seq, _ = qkvz.shape
    n_tiles = DN_CONV_CH // LANES
    return pl.pallas_call(
        _conv_kernel,
        grid=(b, n_tiles),
        in_specs=[pl.BlockSpec((1, seq, LANES), lambda i, c: (i, 0, c)),
                  pl.BlockSpec((DN_CONV_W, LANES), lambda i, c: (0, c))],
        out_specs=pl.BlockSpec((1, seq, LANES), lambda i, c: (i, 0, c)),
        out_shape=jax.ShapeDtypeStruct((b, seq, DN_CONV_CH), BF16),
        scratch_shapes=[pltpu.VMEM((seq + 2 * CONV_PAD_ROWS, LANES), F32)],
        compiler_params=_cparams("parallel", "parallel"),
        name="dn_conv",
    )(qkvz, conv_w)


DELTA_GROUP = 4
DELTA_PAIRS = 2
DELTA_BLK = 2 * DN_CHUNK
DELTA_ROWS = DELTA_GROUP * DN_CHUNK
CHUNK_SHIFT = DN_CHUNK.bit_length() - 1
PAIR_W = 2 * DN_DV
DELTA_SPILL_BYTES = 14 * 1024 * 1024


def _block_diag2(x):
    r = x.shape[0]
    z = jnp.zeros((r, r), x.dtype)
    return jnp.concatenate([jnp.concatenate([x[:, :r], z], axis=1),
                            jnp.concatenate([z, x[:, r:]], axis=1)], axis=0)


def _delta_masks():
    n, c = DELTA_BLK, DN_CHUNK
    ii = lax.broadcasted_iota(jnp.int32, (n, n), 0)
    jj = lax.broadcasted_iota(jnp.int32, (n, n), 1)
    same = jnp.right_shift(ii, CHUNK_SHIFT) == jnp.right_shift(jj, CHUNK_SHIFT)
    ge = jnp.logical_and(same, ii >= jj)
    gt = jnp.logical_and(same, ii > jj)
    le = jnp.logical_and(same, ii <= jj)
    lt = jnp.logical_and(same, ii < jj)
    top = lax.broadcasted_iota(jnp.int32, (n, 1), 0) < c
    r4 = lax.broadcasted_iota(jnp.int32, (c, 4 * c), 0)
    l4 = lax.broadcasted_iota(jnp.int32, (c, 4 * c), 1)
    eye4 = jnp.where(r4 == jnp.bitwise_and(l4, c - 1), 1.0, 0.0).astype(F32)
    rb = lax.broadcasted_iota(jnp.int32, (4 * c, 4 * c), 0)
    lb = lax.broadcasted_iota(jnp.int32, (4 * c, 4 * c), 1)
    bd4 = jnp.right_shift(rb, CHUNK_SHIFT) == jnp.right_shift(lb, CHUNK_SHIFT)
    left = lax.broadcasted_iota(jnp.int32, (c, n), 1) < c
    return dict(eye=ii == jj, eye4=eye4, bd4=bd4, left=left, top=top,
                incl=(ge, le), strict=(gt, lt), cum=(ge, le))


def _block_diag4(x, mk):
    return jnp.where(mk["bd4"], jnp.concatenate([x] * 4, axis=0), jnp.zeros((), x.dtype))


def _delta_local(q_ref, k_ref, v_ref, bg_ref, bases, mk):
    c, n, dv = DN_CHUNK, DELTA_BLK, DN_DV
    units = [(pp, d, b) for pp in range(DELTA_PAIRS) for d in range(2) for b in range(DELTA_GROUP // 2)]
    st = {u: {} for u in units}
    neg_inf = jnp.float32(-jnp.inf)

    def stage_kq():
        for pp, d, b in units:
            s = st[(pp, d, b)]
            rs = pl.ds(pl.multiple_of((bases[d] + b) * n, n), n)
            kd = k_ref[0, rs, pp * DN_DK:(pp + 1) * DN_DK]
            qd = q_ref[0, rs, pp * DN_DK:(pp + 1) * DN_DK]
            kq = _dot_nt(jnp.concatenate([kd, qd], axis=0), kd)
            kk = kq[:n]
            qk = kq[n:]
            kf = kd.astype(F32)
            qf = qd.astype(F32)
            bg = bg_ref[bases[d] + b]
            a_list = []
            for h in range(2):
                row = 8 * pp + 2 * d + h
                brow = bg[row:row + 1, :]
                grow = bg[row + 4:row + 5, :]
                gc_col = jnp.sum(jnp.where(mk["cum"][d], grow, 0.0), axis=1, keepdims=True)
                gc_row = jnp.sum(jnp.where(mk["eye"], gc_col, 0.0), axis=0, keepdims=True)
                bcol = jnp.sum(jnp.where(mk["eye"], brow, 0.0), axis=1, keepdims=True)
                dec = jnp.exp(jnp.where(mk["incl"][d], gc_col - gc_row, neg_inf))
                a = jnp.where(mk["strict"][d], kk * bcol * dec, 0.0)
                a_list.append(a[:c] + a[c:])
                gl = (gc_row[:, c - 1:c], gc_row[:, n - 1:n]) if d == 0 else (gc_row[:, 0:1], gc_row[:, c:c + 1])
                gl_col = jnp.where(mk["top"], gl[0], gl[1])
                e1 = jnp.exp(gc_col)
                e2 = jnp.exp(gl_col - gc_col)
                vf = v_ref[0, rs, (2 * pp + h) * dv:(2 * pp + h + 1) * dv].astype(F32)
                s[h] = dict(rhs=jnp.concatenate([vf * bcol, kf * (bcol * e1)], axis=1).astype(BF16),
                            qe=(qf * e1).astype(BF16), e2=jnp.broadcast_to(e2, (n, dv)),
                            qkd=(qk * dec).astype(BF16), eg=[jnp.exp(gl[0]), jnp.exp(gl[1])])
            s["x"] = -jnp.concatenate(a_list, axis=1)
            s["q"] = mk["eye4"] + s["x"]

    def stage_t_first():
        for u in units:
            s = st[u]
            xb = s["x"].astype(BF16)
            s["x"] = _bdot(xb, _block_diag4(xb, mk))

    def stage_t_mid():
        for u in units:
            s = st[u]
            xb = s["x"].astype(BF16)
            y = _bdot(jnp.concatenate([s["q"].astype(BF16), xb], axis=0), _block_diag4(xb, mk))
            s["q"] = s["q"] + y[:c]
            s["x"] = y[c:]

    def stage_t_last():
        for u in units:
            s = st[u]
            s["q"] = s["q"] + _bdot(s["q"].astype(BF16), _block_diag4(s["x"].astype(BF16), mk))

    def stage_uw():
        for u in units:
            s = st[u]
            for h in range(2):
                th = s["q"][:, h * n:(h + 1) * n].astype(BF16)
                zero = jnp.zeros((), BF16)
                t_bd = jnp.concatenate([jnp.where(mk["left"], th, zero), jnp.where(mk["left"], zero, th)], axis=0)
                uw = _bdot(t_bd, s[h]["rhs"])
                s[h]["u"] = uw[:, :dv]
                s[h]["w"] = uw[:, dv:].astype(BF16)

    stages = [stage_kq, stage_t_first] + [stage_t_mid] * (CHUNK_SHIFT - 2) + [stage_t_last, stage_uw]
    return stages, st


def _delta_store_local(st, slot, u_s, w_s, qe_s, e2_s, qkd_s, eg_s):
    for pp in range(DELTA_PAIRS):
        for d in range(2):
            ch = slot * (2 * DELTA_PAIRS) + 2 * pp + d
            blocks = [st[(pp, d, b)] for b in range(DELTA_GROUP // 2)]
            for name, ref in (("u", u_s), ("w", w_s), ("qe", qe_s), ("e2", e2_s), ("qkd", qkd_s)):
                ref[ch] = jnp.concatenate(
                    [jnp.concatenate([blk[0][name], blk[1][name]], axis=1) for blk in blocks], axis=0)
            for b, blk in enumerate(blocks):
                for half in range(2):
                    j = 2 * b + half
                    eg_s[ch, j:j + 1, :] = jnp.concatenate(
                        [jnp.broadcast_to(blk[h]["eg"][half], (1, DN_DV)) for h in range(2)], axis=1)


def _delta_scan(k_ref, slot, u_s, w_s, qe_s, e2_s, qkd_s, eg_s, s_ref, o_ref, acc_ref, bases):
    c, dv = DN_CHUNK, DN_DV
    n_chains = 2 * DELTA_PAIRS
    first = slot * n_chains
    stages = []
    for jf in range(DELTA_GROUP):
        wq = {}

        def chunk_of(ch, jf=jf):
            return jf if ch % 2 == 0 else DELTA_GROUP - 1 - jf

        def stage1(wq=wq, chunk_of=chunk_of):
            for ch in range(n_chains):
                j = chunk_of(ch)
                rl = slice(j * c, (j + 1) * c)
                lhs = jnp.concatenate([w_s[first + ch, rl, :], qe_s[first + ch, rl, :]], axis=0)
                wq[ch] = _bdot(lhs, _block_diag2(s_ref[ch].astype(BF16)))

        def stage2(wq=wq, chunk_of=chunk_of):
            zpad = jnp.zeros((c, dv), BF16)
            for ch in range(n_chains):
                pp, d = ch // 2, ch % 2
                j = chunk_of(ch)
                rl = slice(j * c, (j + 1) * c)
                r0 = pl.multiple_of(bases[d] * DELTA_BLK + j * c, c)
                v_new = u_s[first + ch, rl, :] - wq[ch][:c]
                vb = v_new.astype(BF16)
                pads = []
                for h in range(2):
                    vh = vb[:, h * dv:(h + 1) * dv]
                    pads.append(jnp.concatenate([vh, zpad] if j % 2 == 0 else [zpad, vh], axis=0))
                v_bd = _block_diag2(jnp.concatenate(pads, axis=1))
                o = wq[ch][c:] + _bdot(qkd_s[first + ch, rl, :], v_bd)
                kc = k_ref[0, pl.ds(r0, c), pp * DN_DK:(pp + 1) * DN_DK]
                decay = eg_s[first + ch, j:j + 1, :]
                s_ref[ch] = s_ref[ch] * decay + _dot_tn(kc, (v_new * e2_s[first + ch, rl, :]).astype(BF16))
                if d == 0:
                    o_ref[0, pl.ds(r0, c), pp * PAIR_W:(pp + 1) * PAIR_W] = o.astype(o_ref.dtype)
                else:
                    acc_ref[pp, pl.ds(r0, c), :] = o.astype(acc_ref.dtype)

        stages += [stage1, stage2]
    return stages


def _delta_kernel(q_ref, k_ref, v_ref, ba_ref, alog_ref, dtb_ref, o_ref,
                  bg_ref, s_ref, acc_ref, u_s, w_s, qe_s, e2_s, qkd_s, eg_s):
    seq = q_ref.shape[1]
    n_groups = seq // DELTA_ROWS
    n = DELTA_BLK

    raw = ba_ref[0, 0]
    xx = raw + dtb_ref[0]
    softplus = jnp.maximum(xx, 0.0) + jnp.log(1.0 + jnp.exp(-jnp.abs(xx)))
    g = -jnp.exp(alog_ref[0]) * softplus
    is_beta = jnp.bitwise_and(lax.broadcasted_iota(jnp.int32, (raw.shape[0], 1), 0), 7) < 4
    bg = jnp.where(is_beta, jax.nn.sigmoid(raw), g)
    for i in range(seq // n):
        bg_ref[i] = bg[:, i * n:(i + 1) * n]
    s_ref[...] = jnp.zeros(s_ref.shape, F32)

    mk = _delta_masks()
    local_scratch = (u_s, w_s, qe_s, e2_s, qkd_s, eg_s)

    def group_bases(t):
        return (t * (DELTA_GROUP // 2), (n_groups - 1 - t) * (DELTA_GROUP // 2))

    stages, st = _delta_local(q_ref, k_ref, v_ref, bg_ref, group_bases(0), mk)
    for stage in stages:
        stage()
    _delta_store_local(st, 0, *local_scratch)

    def trip(t, carry):
        nxt = jnp.minimum(t + 1, n_groups - 1)
        slot = jnp.bitwise_and(t, 1)
        l_stages, l_st = _delta_local(q_ref, k_ref, v_ref, bg_ref, group_bases(nxt), mk)
        s_stages = _delta_scan(k_ref, slot, *local_scratch, s_ref, o_ref, acc_ref, group_bases(t))
        for a, b in zip(l_stages, s_stages):
            a()
            b()
        _delta_store_local(l_st, 1 - slot, *local_scratch)
        return carry

    lax.fori_loop(0, n_groups, trip, 0)

    rows = min(512, seq)
    for r in range(0, seq, rows):
        for pp in range(DELTA_PAIRS):
            cols = slice(pp * PAIR_W, (pp + 1) * PAIR_W)
            o = o_ref[0, r:r + rows, cols].astype(F32) + acc_ref[pp, r:r + rows, :].astype(F32)
            o_ref[0, r:r + rows, cols] = o.astype(o_ref.dtype)


def _dn_delta(qkv_n, ba_r, alog_r, dtb_r):
    b, seq, _ = qkv_n.shape
    assert seq % DELTA_ROWS == 0 and len(_delta_local(None, None, None, None, None, None)[0]) == 2 * DELTA_GROUP
    n_steps = DN_NK // DELTA_PAIRS
    qk_w = DELTA_PAIRS * DN_DK
    v_w = DELTA_PAIRS * PAIR_W
    k_tile0 = DN_K_DIM // qk_w
    v_tile0 = 2 * DN_K_DIM // v_w
    n_rows = 8 * DELTA_PAIRS
    n_chains = 2 * DELTA_PAIRS
    grp = lambda dt: pltpu.VMEM((2 * n_chains, DELTA_ROWS, PAIR_W), dt)
    block_bytes = 2 * (2 * seq * qk_w * 2 + 2 * seq * v_w * 2 + n_rows * seq * 4)
    scratch_bytes = (DELTA_PAIRS * seq * PAIR_W * 2 + 2 * n_chains * DELTA_ROWS * PAIR_W * (4 + 2 + 2 + 4 + 2)
                     + n_chains * DN_DK * PAIR_W * 4 + n_rows * seq * 4 + 2 * n_chains * 8 * PAIR_W * 4)
    vmem = block_bytes + scratch_bytes + DELTA_SPILL_BYTES
    return pl.pallas_call(
        _delta_kernel,
        grid=(b, n_steps),
        in_specs=[pl.BlockSpec((1, seq, qk_w), lambda i, p: (i, 0, p)),
                  pl.BlockSpec((1, seq, qk_w), lambda i, p: (i, 0, k_tile0 + p)),
                  pl.BlockSpec((1, seq, v_w), lambda i, p: (i, 0, v_tile0 + p)),
                  pl.BlockSpec((1, 1, n_rows, seq), lambda i, p: (i, p, 0, 0)),
                  pl.BlockSpec((1, n_rows, 1), lambda i, p: (p, 0, 0)),
                  pl.BlockSpec((1, n_rows, 1), lambda i, p: (p, 0, 0))],
        out_specs=pl.BlockSpec((1, seq, v_w), lambda i, p: (i, 0, p)),
        out_shape=jax.ShapeDtypeStruct((b, seq, DN_V_DIM), BF16),
        scratch_shapes=[pltpu.VMEM((seq // DELTA_BLK, n_rows, DELTA_BLK), F32),
                        pltpu.VMEM((n_chains, DN_DK, PAIR_W), F32),
                        pltpu.VMEM((DELTA_PAIRS, seq, PAIR_W), BF16),
                        grp(F32), grp(BF16), grp(BF16), grp(F32), grp(BF16),
                        pltpu.VMEM((2 * n_chains, 8, PAIR_W), F32)],
        compiler_params=_cparams("parallel", "parallel", vmem=vmem),
        name="dn_delta",
    )(qkv_n, qkv_n, qkv_n, ba_r, alog_r, dtb_r)


def _step_rows(t):
    n_steps = DN_NK // DELTA_PAIRS
    t = t.astype(F32).reshape(2, n_steps, DELTA_PAIRS, 2).transpose(1, 2, 0, 3).reshape(n_steps, DELTA_PAIRS, 4)
    return jnp.concatenate([jnp.zeros_like(t), t], axis=2).reshape(n_steps, 8 * DELTA_PAIRS, 1)


def _gated_proj_ln_kernel(o_ref, z_ref, nw_ref, w_ref, x_ref, g_ref, b_ref, out_ref):
    nw = nw_ref[...]
    cols = []
    for h in range(DN_NV):
        sl = slice(h * DN_DV, (h + 1) * DN_DV)
        o = o_ref[:, sl].astype(F32)
        ms = jnp.mean(o * o, axis=-1, keepdims=True)
        cols.append((o * lax.rsqrt(ms + RMS_EPS) * nw * _silu(z_ref[:, sl].astype(F32))).astype(BF16))
    y = _bdot(jnp.concatenate(cols, axis=1), w_ref[...])
    out_ref[...] = _layernorm(DEEPNORM_ALPHA * x_ref[...] + y, g_ref[...], b_ref[...])


def _gated_proj_ln(o, qkvz, norm_w, w, x, g, b, tm=512):
    t, k = o.shape
    d = w.shape[1]
    tm = min(tm, t)
    z_tile = DN_CONV_CH // k
    return pl.pallas_call(
        _gated_proj_ln_kernel,
        grid=(t // tm,),
        in_specs=[pl.BlockSpec((tm, k), lambda i: (i, 0)),
                  pl.BlockSpec((tm, k), lambda i: (i, z_tile)),
                  pl.BlockSpec((1, DN_DV), lambda i: (0, 0)),
                  pl.BlockSpec((k, d), lambda i: (0, 0)),
                  pl.BlockSpec((tm, d), lambda i: (i, 0)),
                  pl.BlockSpec((1, d), lambda i: (0, 0)),
                  pl.BlockSpec((1, d), lambda i: (0, 0))],
        out_specs=pl.BlockSpec((tm, d), lambda i: (i, 0)),
        out_shape=jax.ShapeDtypeStruct((t, d), F32),
        compiler_params=_cparams("parallel"),
        name="gated_proj_ln",
    )(o, qkvz, norm_w.reshape(1, DN_DV), w, x, g.reshape(1, d), b.reshape(1, d))


def _deltanet_layer(xf, bsz, seq, w_in, conv_w, a_log, dt_bias, norm_w, w_out, ln_g, ln_b):
    t = xf.shape[0]
    n_qkvz = DN_CONV_CH + DN_V_DIM
    n_steps = DN_NK // DELTA_PAIRS
    w_bf = w_in.astype(BF16)
    qkvz = _matmul(xf, w_bf[:, :n_qkvz], BF16, tm=1024, tn=1536)
    ba = _matmul(xf, w_bf[:, n_qkvz:], F32, tm=1024, tn=4 * DN_NV)
    ba_r = ba.reshape(bsz, seq, 2, 2, n_steps, DELTA_PAIRS, 2).transpose(0, 4, 5, 2, 3, 6, 1)
    ba_r = ba_r.reshape(bsz, n_steps, 8 * DELTA_PAIRS, seq)
    qkv_n = _dn_conv(qkvz.reshape(bsz, seq, n_qkvz), conv_w)
    o = _dn_delta(qkv_n, ba_r, _step_rows(a_log), _step_rows(dt_bias))
    return _gated_proj_ln(o.reshape(t, DN_V_DIM), qkvz, norm_w, w_out.astype(BF16), xf, ln_g, ln_b)


MLA_QK = MLA_NOPE + MLA_ROPE
MLA_IN_EXT = MLA_QR + MLA_KVR + 2 * LANES
MLA_Q_EXT = MLA_NOPE + 2 * MLA_ROPE
MLA_V_EXT = MLA_V + 16
LOG2E = 1.4426950408889634


def _rope_kernel(pos_ref, invf_ref, sign_ref, cs_ref, sn_ref):
    ang = invf_ref[...] * pos_ref[0]
    cs_ref[0] = jnp.cos(ang)
    sn_ref[0] = jnp.sin(ang) * sign_ref[...]


def _rope_tables(positions):
    b, seq = positions.shape
    half = MLA_ROPE // 2
    inv_freq = 1.0 / (MLA_ROPE_THETA ** (jnp.arange(0, MLA_ROPE, 2, dtype=F32) / MLA_ROPE))
    invf = jnp.concatenate([inv_freq, inv_freq]).reshape(MLA_ROPE, 1)
    sign = jnp.concatenate([-jnp.ones((half,), F32), jnp.ones((half,), F32)]).reshape(MLA_ROPE, 1)
    out = jax.ShapeDtypeStruct((b, MLA_ROPE, seq), F32)
    return pl.pallas_call(
        _rope_kernel,
        grid=(b,),
        in_specs=[pl.BlockSpec((1, 1, seq), lambda i: (i, 0, 0)),
                  pl.BlockSpec((MLA_ROPE, 1), lambda i: (0, 0)),
                  pl.BlockSpec((MLA_ROPE, 1), lambda i: (0, 0))],
        out_specs=[pl.BlockSpec((1, MLA_ROPE, seq), lambda i: (i, 0, 0)),
                   pl.BlockSpec((1, MLA_ROPE, seq), lambda i: (i, 0, 0))],
        out_shape=[out, out],
        compiler_params=_cparams("parallel"),
        name="rope_tables",
    )(positions.astype(F32).reshape(b, 1, seq), invf, sign)


def _rms(x, w):
    return x * lax.rsqrt(jnp.mean(x * x, axis=-1, keepdims=True) + RMS_EPS) * w


def _mla_in_kernel(x_ref, w_ref, qn_ref, kvn_ref, cs_ref, sn_ref, cq_ref, ckv_ref, kr_ref):
    h = _bdot(x_ref[...].astype(BF16), w_ref[...])
    cq_ref[...] = _rms(h[:, :MLA_QR], qn_ref[...]).astype(cq_ref.dtype)
    ckv_ref[...] = _rms(h[:, MLA_QR:MLA_QR + MLA_KVR], kvn_ref[...]).astype(ckv_ref.dtype)
    r0 = MLA_QR + MLA_KVR
    kr = h[:, r0:r0 + MLA_ROPE] * cs_ref[...] + h[:, r0 + LANES:r0 + LANES + MLA_ROPE] * sn_ref[...]
    kr_ref[...] = kr.astype(kr_ref.dtype)


def _mla_in(xf, w_ext, q_norm, kv_norm, cs, sn, tm=512):
    t, d = xf.shape
    tm = min(tm, t)
    row = lambda i: (i, 0)
    fix = lambda i: (0, 0)
    return pl.pallas_call(
        _mla_in_kernel,
        grid=(t // tm,),
        in_specs=[pl.BlockSpec((tm, d), row),
                  pl.BlockSpec((d, MLA_IN_EXT), fix),
                  pl.BlockSpec((1, MLA_QR), fix),
                  pl.BlockSpec((1, MLA_KVR), fix),
                  pl.BlockSpec((tm, MLA_ROPE), row),
                  pl.BlockSpec((tm, MLA_ROPE), row)],
        out_specs=[pl.BlockSpec((tm, MLA_QR), row),
                   pl.BlockSpec((tm, MLA_KVR), row),
                   pl.BlockSpec((tm, MLA_ROPE), row)],
        out_shape=[jax.ShapeDtypeStruct((t, MLA_QR), BF16),
                   jax.ShapeDtypeStruct((t, MLA_KVR), BF16),
                   jax.ShapeDtypeStruct((t, MLA_ROPE), BF16)],
        compiler_params=_cparams("parallel"),
        name="mla_in",
    )(xf, w_ext, q_norm.reshape(1, MLA_QR), kv_norm.reshape(1, MLA_KVR), cs, sn)


def _mla_q_kernel(cq_ref, w_ref, cs_ref, sn_ref, q_ref):
    acc = _dot_nt(w_ref[...], cq_ref[...])
    cs = cs_ref[0]
    sn = sn_ref[0]
    scale = MLA_QK ** -0.5 * LOG2E
    for h in range(MLA_H):
        a0 = h * MLA_Q_EXT
        o0 = h * MLA_QK
        q_ref[0, o0:o0 + MLA_NOPE, :] = (acc[a0:a0 + MLA_NOPE] * scale).astype(q_ref.dtype)
        x = acc[a0 + MLA_NOPE:a0 + MLA_NOPE + MLA_ROPE]
        xs = acc[a0 + MLA_NOPE + MLA_ROPE:a0 + MLA_Q_EXT]
        q_ref[0, o0 + MLA_NOPE:o0 + MLA_QK, :] = ((x * cs + xs * sn) * scale).astype(q_ref.dtype)


def _mla_q(cq, wq_t, cs_t, sn_t, bsz, seq, tl=512):
    tl = min(tl, seq)
    nl = seq // tl
    return pl.pallas_call(
        _mla_q_kernel,
        grid=(bsz, nl),
        in_specs=[pl.BlockSpec((tl, MLA_QR), lambda b, i: (b * nl + i, 0)),
                  pl.BlockSpec((MLA_H * MLA_Q_EXT, MLA_QR), lambda b, i: (0, 0)),
                  pl.BlockSpec((1, MLA_ROPE, tl), lambda b, i: (b, 0, i)),
                  pl.BlockSpec((1, MLA_ROPE, tl), lambda b, i: (b, 0, i))],
        out_specs=pl.BlockSpec((1, MLA_H * MLA_QK, tl), lambda b, i: (b, 0, i)),
        out_shape=jax.ShapeDtypeStruct((bsz, MLA_H * MLA_QK, seq), BF16),
        compiler_params=_cparams("parallel", "parallel"),
        name="mla_q",
    )(cq, wq_t, cs_t, sn_t)


def _mla_kv_kernel(ckv_ref, kr_ref, wk_ref, wv_ref, k_ref, v_ref):
    ckv = ckv_ref[...]
    kn = _bdot(ckv, wk_ref[...])
    kr = kr_ref[...]
    for h in range(MLA_H):
        k_ref[0, h] = jnp.concatenate(
            [kn[:, h * MLA_NOPE:(h + 1) * MLA_NOPE].astype(k_ref.dtype), kr], axis=1)
    v_t = _dot_nt(wv_ref[...], ckv).astype(v_ref.dtype)
    ones = jnp.ones((MLA_V_EXT - MLA_V, v_t.shape[1]), v_ref.dtype)
    for h in range(MLA_H):
        v_ref[0, h * MLA_V_EXT:h * MLA_V_EXT + MLA_V, :] = v_t[h * MLA_V:(h + 1) * MLA_V]
        v_ref[0, h * MLA_V_EXT + MLA_V:(h + 1) * MLA_V_EXT, :] = ones


def _mla_kv(ckv, kr, wk, wv_t, bsz, seq, tl=512):
    tl = min(tl, seq)
    nl = seq // tl
    return pl.pallas_call(
        _mla_kv_kernel,
        grid=(bsz, nl),
        in_specs=[pl.BlockSpec((tl, MLA_KVR), lambda b, i: (b * nl + i, 0)),
                  pl.BlockSpec((tl, MLA_ROPE), lambda b, i: (b * nl + i, 0)),
                  pl.BlockSpec((MLA_KVR, MLA_H * MLA_NOPE), lambda b, i: (0, 0)),
                  pl.BlockSpec((MLA_H * MLA_V, MLA_KVR), lambda b, i: (0, 0))],
        out_specs=[pl.BlockSpec((1, MLA_H, tl, MLA_QK), lambda b, i: (b, 0, i, 0)),
                   pl.BlockSpec((1, MLA_H * MLA_V_EXT, tl), lambda b, i: (b, 0, i))],
        out_shape=[jax.ShapeDtypeStruct((bsz, MLA_H, seq, MLA_QK), BF16),
                   jax.ShapeDtypeStruct((bsz, MLA_H * MLA_V_EXT, seq), BF16)],
        compiler_params=_cparams("parallel", "parallel"),
        name="mla_kv",
    )(ckv, kr, wk, wv_t)


ATTN_TQ = 512
ATTN_TK = 512


def _attn_kernel(q_ref, k_ref, v_ref, o_ref):
    q_t = q_ref[0]
    tq = q_t.shape[1]
    seq = k_ref.shape[2]
    tk = min(ATTN_TK, seq)
    m = jnp.full((1, tq), -jnp.inf, F32)
    acc = jnp.zeros((MLA_V_EXT, tq), F32)
    n_k = seq // tk
    scores = lambda j: _bdot(k_ref[0, 0, j * tk:(j + 1) * tk, :], q_t)
    s_next = scores(0)
    for j in range(n_k):
        s = s_next
        if j + 1 < n_k:
            s_next = scores(j + 1)
        m_new = jnp.maximum(m, jnp.max(s, axis=0, keepdims=True))
        p = jnp.exp2(s - m_new).astype(BF16)
        acc = jnp.exp2(m - m_new) * acc + _bdot(v_ref[0, :, j * tk:(j + 1) * tk], p)
        m = m_new
    o_ref[0] = (acc[:MLA_V] / acc[MLA_V:MLA_V + 1]).T.astype(o_ref.dtype)


def _mla_attn(q_t, k, v_t, bsz, seq):
    tq = min(ATTN_TQ, seq)
    return pl.pallas_call(
        _attn_kernel,
        grid=(bsz, MLA_H, seq // tq),
        in_specs=[pl.BlockSpec((1, MLA_QK, tq), lambda b, h, i: (b, h, i)),
                  pl.BlockSpec((1, 1, seq, MLA_QK), lambda b, h, i: (b, h, 0, 0)),
                  pl.BlockSpec((1, MLA_V_EXT, seq), lambda b, h, i: (b, h, 0))],
        out_specs=pl.BlockSpec((1, tq, MLA_V), lambda b, h, i: (b, i, h)),
        out_shape=jax.ShapeDtypeStruct((bsz, seq, MLA_H * MLA_V), BF16),
        compiler_params=_cparams("parallel", "parallel", "parallel"),
        name="mla_attn",
    )(q_t, k, v_t)


def _swap_halves(t):
    half = t.shape[-1] // 2
    return jnp.concatenate([t[..., half:], t[..., :half]], axis=-1)


def _mla_layer(xf, bsz, seq, cs, sn, cs_t, sn_t, w_in, q_norm, w_qb, kv_norm, w_kvb, w_out, ln_g, ln_b):
    t, d = xf.shape
    r0 = MLA_QR + MLA_KVR
    w_rope = w_in[:, r0:]
    pad = jnp.zeros((d, LANES - MLA_ROPE), w_in.dtype)
    w_ext = jnp.concatenate([w_in, pad, _swap_halves(w_rope), pad], axis=1).astype(BF16)
    wq = w_qb.reshape(MLA_QR, MLA_H, MLA_QK)
    wq_rope = wq[:, :, MLA_NOPE:]
    wq_t = jnp.concatenate([wq, _swap_halves(wq_rope)], axis=-1).reshape(MLA_QR, MLA_H * MLA_Q_EXT).T.astype(BF16)
    wkv = w_kvb.reshape(MLA_KVR, MLA_H, MLA_NOPE + MLA_V)
    wk = wkv[:, :, :MLA_NOPE].reshape(MLA_KVR, MLA_H * MLA_NOPE).astype(BF16)
    wv_t = wkv[:, :, MLA_NOPE:].reshape(MLA_KVR, MLA_H * MLA_V).T.astype(BF16)

    cq, ckv, kr = _mla_in(xf, w_ext, q_norm, kv_norm, cs, sn)
    q_t = _mla_q(cq, wq_t, cs_t, sn_t, bsz, seq)
    k, v_t = _mla_kv(ckv, kr, wk, wv_t, bsz, seq)
    o = _mla_attn(q_t, k, v_t, bsz, seq)
    return _proj_ln(o.reshape(t, MLA_H * MLA_V), w_out.astype(BF16), xf, ln_g, ln_b)


ROUTER_TILE = LANES // N_EXPERTS


def _lane_roll(x, shift):
    return pltpu.roll(x, shift % LANES, 1)


def _ind(mask):
    return jnp.where(mask, 1.0, 0.0)


def _route(logits, bias):
    scores = jax.nn.sigmoid(logits)
    sel = scores + bias
    lane = lax.broadcasted_iota(jnp.int32, sel.shape, 1)
    e_in_g = jnp.bitwise_and(lane, EXPERTS_PER_GROUP - 1)
    g_idx = jnp.right_shift(jnp.bitwise_and(lane, N_EXPERTS - 1), EXPERTS_PER_GROUP.bit_length() - 1)

    def group_nbr(x, k):
        return jnp.where(e_in_g + k < EXPERTS_PER_GROUP,
                         _lane_roll(x, -k), _lane_roll(x, EXPERTS_PER_GROUP - k))

    rank = jnp.zeros(sel.shape, F32)
    for k in range(1, EXPERTS_PER_GROUP):
        other = group_nbr(sel, k)
        other_first = e_in_g + k >= EXPERTS_PER_GROUP
        rank = rank + jnp.where(other_first, _ind(other >= sel), _ind(other > sel))
    top2 = rank < 2.0
    t = jnp.where(top2, sel, 0.0)
    gs = t
    for k in range(1, EXPERTS_PER_GROUP):
        gs = gs + group_nbr(t, k)
    lost = jnp.zeros(sel.shape, F32)
    for k in range(1, N_GROUPS):
        other = _lane_roll(gs, -k * EXPERTS_PER_GROUP)
        other_first = g_idx + k >= N_GROUPS
        lost = lost + jnp.where(other_first, _ind(other >= gs), _ind(other > gs))
    won = lost < 0.5
    w = jnp.where(top2, jnp.where(won, scores, 0.0), 0.0)
    tot = w
    for sh in (8, 4, 2, 1):
        tot = tot + _lane_roll(tot, sh)
    return w / tot, _ind(won)


MOE_BLOCK = 128


def _router_weights(router_w):
    hi, lo = _split_bf16(jnp.tile(router_w.astype(F32), (1, ROUTER_TILE)))
    return jnp.concatenate([hi, lo], axis=1)


def _split_bf16(v):
    hi = v.astype(BF16)
    return hi, (v - hi.astype(F32)).astype(BF16)


def _moe_kernel(x_ref, rw_ref, rb_ref, wg_ref, wu_ref, wd_ref, g_ref, b_ref, o_ref,
                xs_ref, pt_ref, gate_ref, acc_ref, seg_ref):
    j = pl.program_id(1)
    tm = x_ref.shape[0]

    @pl.when(j == 0)
    def _():
        x = x_ref[...]
        xh, xl = _split_bf16(x)
        rw = rw_ref[...]
        lg = _bdot(xh, rw)
        logits = lg[:, :LANES] + lg[:, LANES:] + _bdot(xl, rw[:, :LANES])
        gates, won = _route(logits, rb_ref[...])
        lane = lax.broadcasted_iota(jnp.int32, (1, LANES), 1)
        first = jnp.logical_and(jnp.bitwise_and(lane, EXPERTS_PER_GROUP - 1) == 0, lane < N_EXPERTS)
        onehot = jnp.where(first, won, 0.0)
        ri = lax.broadcasted_iota(jnp.int32, (tm, tm), 0)
        ci = lax.broadcasted_iota(jnp.int32, (tm, tm), 1)
        ltri = jnp.where(ri >= ci, 1.0, 0.0).astype(BF16)
        prefix = _bdot(ltri, onehot.astype(BF16))
        total = prefix[tm - 1:tm, :]
        cnt = [total[:, g * EXPERTS_PER_GROUP:g * EXPERTS_PER_GROUP + 1] for g in range(N_GROUPS)]
        off = [jnp.zeros((1, 1), F32)]
        for g in range(1, N_GROUPS):
            off.append(off[-1] + cnt[g - 1])
        pos = jnp.zeros((tm, 1), F32)
        for g in range(N_GROUPS):
            col = slice(g * EXPERTS_PER_GROUP, g * EXPERTS_PER_GROUP + 1)
            pos = pos + onehot[:, col] * (off[g] + prefix[:, col] - 1.0)
        hit = pos == ci.astype(F32)
        pt_ref[...] = jnp.where(hit, 1.0, 0.0).astype(BF16)
        pos_row = jnp.sum(jnp.where(ri == ci, pos, 0.0), axis=0, keepdims=True)
        p = jnp.where(pos_row == ri.astype(F32), 1.0, 0.0).astype(BF16)
        xs_ref[...] = _bdot(p, xh).astype(BF16)
        gh, gl = _split_bf16(gates)
        g_sorted = _bdot(p, gh) + _bdot(p, gl)
        for g in range(N_GROUPS):
            gate_ref[g] = _lane_roll(g_sorted, -g * EXPERTS_PER_GROUP)
            seg_ref[g] = off[g][0, 0].astype(jnp.int32)
            seg_ref[N_GROUPS + g] = cnt[g][0, 0].astype(jnp.int32)
        acc_ref[...] = jnp.zeros(acc_ref.shape, F32)

    seg_lo = seg_ref[j]
    seg_hi = seg_lo + seg_ref[N_GROUPS + j]
    for r in range(0, tm, MOE_BLOCK):
        @pl.when(jnp.logical_and(seg_lo < r + MOE_BLOCK, seg_hi > r))
        def _(r=r):
            xb = xs_ref[r:r + MOE_BLOCK, :]
            gates = gate_ref[j, r:r + MOE_BLOCK, :]
            hid = []
            for e in range(EXPERTS_PER_GROUP):
                hg = _bdot(xb, wg_ref[e])
                hu = _bdot(xb, wu_ref[e])
                hid.append((_silu(hg) * hu * gates[:, e:e + 1]).astype(BF16))
            acc_ref[r:r + MOE_BLOCK, :] += _bdot(jnp.concatenate(hid, axis=1), wd_ref[...])

    @pl.when(j == N_GROUPS - 1)
    def _():
        y = _bdot(pt_ref[...], acc_ref[...].astype(BF16))
        o_ref[...] = _layernorm(DEEPNORM_ALPHA * x_ref[...] + y, g_ref[...], b_ref[...])


def _moe_layer(xf, rw_t, rb_t, w_gate, w_up, w_down, ln_g, ln_b, tm=512):
    t, d = xf.shape
    tm = min(tm, t)
    f = D_FF_EXPERT
    gf = EXPERTS_PER_GROUP * f
    row = lambda i, j: (i, 0)
    fix = lambda i, j: (0, 0)
    return pl.pallas_call(
        _moe_kernel,
        grid=(t // tm, N_GROUPS),
        in_specs=[pl.BlockSpec((tm, d), row),
                  pl.BlockSpec((d, 2 * LANES), fix),
                  pl.BlockSpec((1, LANES), fix),
                  pl.BlockSpec((EXPERTS_PER_GROUP, d, f), lambda i, j: (j, 0, 0)),
                  pl.BlockSpec((EXPERTS_PER_GROUP, d, f), lambda i, j: (j, 0, 0)),
                  pl.BlockSpec((gf, d), lambda i, j: (j, 0)),
                  pl.BlockSpec((1, d), fix),
                  pl.BlockSpec((1, d), fix)],
        out_specs=pl.BlockSpec((tm, d), row),
        out_shape=jax.ShapeDtypeStruct((t, d), F32),
        scratch_shapes=[pltpu.VMEM((tm, d), BF16),
                        pltpu.VMEM((tm, tm), BF16),
                        pltpu.VMEM((N_GROUPS, tm, LANES), F32),
                        pltpu.VMEM((tm, d), F32),
                        pltpu.SMEM((2 * N_GROUPS,), jnp.int32)],
        compiler_params=_cparams("parallel", "arbitrary"),
        name="moe",
    )(xf, rw_t, rb_t, w_gate.astype(BF16), w_up.astype(BF16),
      w_down.astype(BF16).reshape(N_EXPERTS * f, d), ln_g.reshape(1, d), ln_b.reshape(1, d))


def kernel(x, positions, dn_w_in, dn_conv_w, dn_a_log, dn_dt_bias, dn_norm_w, dn_w_out, mla_w_in, mla_q_norm, mla_w_qb, mla_kv_norm, mla_w_kvb, mla_w_out, ln1_g, ln1_b, ln2_g, ln2_b, router_w, router_bias, moe_w_gate, moe_w_up, moe_w_down):
    bsz, seq, d = x.shape
    t = bsz * seq
    xf = x.reshape(t, d)
    cs_t, sn_t = _rope_tables(positions)
    cs = cs_t.transpose(0, 2, 1).reshape(t, MLA_ROPE)
    sn = sn_t.transpose(0, 2, 1).reshape(t, MLA_ROPE)
    rw_t = _router_weights(router_w)
    rb_t = jnp.tile(router_bias.astype(F32), (ROUTER_TILE,)).reshape(1, LANES)
    for i in range(DEPTH):
        j = i // 2
        if i % 2 == 0:
            xf = _deltanet_layer(xf, bsz, seq, dn_w_in[j], dn_conv_w[j], dn_a_log[j], dn_dt_bias[j],
                                 dn_norm_w[j], dn_w_out[j], ln1_g[i], ln1_b[i])
        else:
            xf = _mla_layer(xf, bsz, seq, cs, sn, cs_t, sn_t, mla_w_in[j], mla_q_norm[j], mla_w_qb[j],
                            mla_kv_norm[j], mla_w_kvb[j], mla_w_out[j], ln1_g[i], ln1_b[i])
        xf = _moe_layer(xf, rw_t, rb_t, moe_w_gate[i], moe_w_up[i], moe_w_down[i], ln2_g[i], ln2_b[i])
    return xf.reshape(bsz, seq, d)
```

```python
import functools

import jax
import jax.numpy as jnp
from jax import lax
from jax.experimental import pallas as pl
from jax.experimental.pallas import tpu as pltpu

F32 = jnp.float32
BF16 = jnp.bfloat16

DEPTH = 4
DN_NK = 8
DN_NV = 16
DN_DK = 128
DN_DV = 128
DN_K_DIM = DN_NK * DN_DK
DN_V_DIM = DN_NV * DN_DV
DN_CONV_CH = 2 * DN_K_DIM + DN_V_DIM
DN_CONV_W = 5
DN_CHUNK = 64
MLA_H = 8
MLA_NOPE = 128
MLA_ROPE = 64
MLA_V = 128
MLA_QR = 384
MLA_KVR = 256
MLA_ROPE_THETA = 10000.0
N_EXPERTS = 16
N_GROUPS = 4
EXPERTS_PER_GROUP = N_EXPERTS // N_GROUPS
D_FF_EXPERT = 256
DEEPNORM_ALPHA = (2 * DEPTH) ** 0.25
LN_EPS = 1e-5
RMS_EPS = 1e-6

LANES = 128
VMEM_LIMIT = 48 * 1024 * 1024


def _cparams(*sems, vmem=VMEM_LIMIT):
    return pltpu.CompilerParams(dimension_semantics=sems, vmem_limit_bytes=vmem)


def _bdot(a, b):
    return jnp.dot(a, b, preferred_element_type=F32)


def _dot_nt(a, b):
    return lax.dot_general(a, b, (((1,), (1,)), ((), ())), preferred_element_type=F32)


def _dot_tn(a, b):
    return lax.dot_general(a, b, (((0,), (0,)), ((), ())), preferred_element_type=F32)


def _silu(x):
    return x * jax.nn.sigmoid(x)


def _mm_kernel(x_ref, w_ref, o_ref):
    o_ref[...] = _bdot(x_ref[...].astype(BF16), w_ref[...]).astype(o_ref.dtype)


def _matmul(x, w, out_dtype, tm, tn):
    t, k = x.shape
    n = w.shape[1]
    tm = min(tm, t)
    return pl.pallas_call(
        _mm_kernel,
        grid=(n // tn, t // tm),
        in_specs=[pl.BlockSpec((tm, k), lambda j, i: (i, 0)),
                  pl.BlockSpec((k, tn), lambda j, i: (0, j))],
        out_specs=pl.BlockSpec((tm, tn), lambda j, i: (i, j)),
        out_shape=jax.ShapeDtypeStruct((t, n), out_dtype),
        compiler_params=_cparams("parallel", "parallel"),
        name="proj",
    )(x, w)


def _layernorm(h, g, b):
    mu = jnp.mean(h, axis=-1, keepdims=True)
    hc = h - mu
    var = jnp.mean(hc * hc, axis=-1, keepdims=True)
    return hc * lax.rsqrt(var + LN_EPS) * g + b


def _proj_ln_kernel(a_ref, w_ref, x_ref, g_ref, b_ref, o_ref):
    y = _bdot(a_ref[...], w_ref[...])
    o_ref[...] = _layernorm(DEEPNORM_ALPHA * x_ref[...] + y, g_ref[...], b_ref[...])


def _proj_ln(a, w, x, g, b, tm=512):
    t, k = a.shape
    d = w.shape[1]
    tm = min(tm, t)
    return pl.pallas_call(
        _proj_ln_kernel,
        grid=(t // tm,),
        in_specs=[pl.BlockSpec((tm, k), lambda i: (i, 0)),
                  pl.BlockSpec((k, d), lambda i: (0, 0)),
                  pl.BlockSpec((tm, d), lambda i: (i, 0)),
                  pl.BlockSpec((1, d), lambda i: (0, 0)),
                  pl.BlockSpec((1, d), lambda i: (0, 0))],
        out_specs=pl.BlockSpec((tm, d), lambda i: (i, 0)),
        out_shape=jax.ShapeDtypeStruct((t, d), F32),
        compiler_params=_cparams("parallel"),
        name="proj_ln",
    )(a, w, x, g.reshape(1, d), b.reshape(1, d))


CONV_PAD_ROWS = 8
CONV_ROWS = 512


def _conv_kernel(x_ref, w_ref, o_ref, pad_ref):
    c = pl.program_id(1)
    seq = x_ref.shape[1]
    zeros = jnp.zeros((CONV_PAD_ROWS, LANES), F32)
    pad_ref[0:CONV_PAD_ROWS, :] = zeros
    pad_ref[seq + CONV_PAD_ROWS:seq + 2 * CONV_PAD_ROWS, :] = zeros
    pad_ref[CONV_PAD_ROWS:seq + CONV_PAD_ROWS, :] = x_ref[0].astype(F32)
    w = w_ref[...]
    is_q = c < DN_NK
    is_v = c >= 2 * DN_NK
    rows = min(CONV_ROWS, seq)
    half = DN_CONV_W // 2
    for r in range(0, seq, rows):
        acc = None
        for j in range(DN_CONV_W):
            lo = CONV_PAD_ROWS + r + j - half
            term = w[j:j + 1, :] * pad_ref[lo:lo + rows, :]
            acc = term if acc is None else acc + term
        y = _silu(acc)
        nrm = lax.rsqrt(jnp.sum(y * y, axis=-1, keepdims=True) + RMS_EPS)
        nrm = nrm * jnp.where(is_q, DN_DK ** -0.5, 1.0)
        scale = jnp.where(is_v, jnp.ones_like(nrm), nrm)
        o_ref[0, r:r + rows, :] = (y * scale).astype(o_ref.dtype)


def _dn_conv(qkvz, conv_w):
    b, seq, _ = qkvz.shape
    n_tiles = DN_CONV_CH // LANES
    return pl.pallas_call(
        _conv_kernel,
        grid=(b, n_tiles),
        in_specs=[pl.BlockSpec((1, seq, LANES), lambda i, c: (i, 0, c)),
                  pl.BlockSpec((DN_CONV_W, LANES), lambda i, c: (0, c))],
        out_specs=pl.BlockSpec((1, seq, LANES), lambda i, c: (i, 0, c)),
        out_shape=jax.ShapeDtypeStruct((b, seq, DN_CONV_CH), BF16),
        scratch_shapes=[pltpu.VMEM((seq + 2 * CONV_PAD_ROWS, LANES), F32)],
        compiler_params=_cparams("parallel", "parallel"),
        name="dn_conv",
    )(qkvz, conv_w)


DELTA_GROUP = 4
DELTA_PAIRS = 2
DELTA_BLK = 2 * DN_CHUNK
DELTA_ROWS = DELTA_GROUP * DN_CHUNK
CHUNK_SHIFT = DN_CHUNK.bit_length() - 1
PAIR_W = 2 * DN_DV
DELTA_SPILL_BYTES = 14 * 1024 * 1024


def _block_diag2(x):
    r = x.shape[0]
    z = jnp.zeros((r, r), x.dtype)
    return jnp.concatenate([jnp.concatenate([x[:, :r], z], axis=1),
                            jnp.concatenate([z, x[:, r:]], axis=1)], axis=0)


def _delta_masks():
    n, c = DELTA_BLK, DN_CHUNK
    ii = lax.broadcasted_iota(jnp.int32, (n, n), 0)
    jj = lax.broadcasted_iota(jnp.int32, (n, n), 1)
    same = jnp.right_shift(ii, CHUNK_SHIFT) == jnp.right_shift(jj, CHUNK_SHIFT)
    ge = jnp.logical_and(same, ii >= jj)
    gt = jnp.logical_and(same, ii > jj)
    le = jnp.logical_and(same, ii <= jj)
    lt = jnp.logical_and(same, ii < jj)
    top = lax.broadcasted_iota(jnp.int32, (n, 1), 0) < c
    r4 = lax.broadcasted_iota(jnp.int32, (c, 4 * c), 0)
    l4 = lax.broadcasted_iota(jnp.int32, (c, 4 * c), 1)
    eye4 = jnp.where(r4 == jnp.bitwise_and(l4, c - 1), 1.0, 0.0).astype(F32)
    rb = lax.broadcasted_iota(jnp.int32, (4 * c, 4 * c), 0)
    lb = lax.broadcasted_iota(jnp.int32, (4 * c, 4 * c), 1)
    bd4 = jnp.right_shift(rb, CHUNK_SHIFT) == jnp.right_shift(lb, CHUNK_SHIFT)
    left = lax.broadcasted_iota(jnp.int32, (c, n), 1) < c
    return dict(eye=ii == jj, eye4=eye4, bd4=bd4, left=left, top=top,
                incl=(ge, le), strict=(gt, lt), cum=(ge, le))


def _block_diag4(x, mk):
    return jnp.where(mk["bd4"], jnp.concatenate([x] * 4, axis=0), jnp.zeros((), x.dtype))


def _delta_local(q_ref, k_ref, v_ref, bg_ref, bases, mk):
    c, n, dv = DN_CHUNK, DELTA_BLK, DN_DV
    units = [(pp, d, b) for pp in range(DELTA_PAIRS) for d in range(2) for b in range(DELTA_GROUP // 2)]
    st = {u: {} for u in units}
    neg_inf = jnp.float32(-jnp.inf)

    def stage_kq():
        for pp, d, b in units:
            s = st[(pp, d, b)]
            rs = pl.ds(pl.multiple_of((bases[d] + b) * n, n), n)
            kd = k_ref[0, rs, pp * DN_DK:(pp + 1) * DN_DK]
            qd = q_ref[0, rs, pp * DN_DK:(pp + 1) * DN_DK]
            kq = _dot_nt(jnp.concatenate([kd, qd], axis=0), kd)
            kk = kq[:n]
            qk = kq[n:]
            kf = kd.astype(F32)
            qf = qd.astype(F32)
            bg = bg_ref[bases[d] + b]
            a_list = []
            for h in range(2):
                row = 8 * pp + 2 * d + h
                brow = bg[row:row + 1, :]
                grow = bg[row + 4:row + 5, :]
                gc_col = jnp.sum(jnp.where(mk["cum"][d], grow, 0.0), axis=1, keepdims=True)
                gc_row = jnp.sum(jnp.where(mk["eye"], gc_col, 0.0), axis=0, keepdims=True)
                bcol = jnp.sum(jnp.where(mk["eye"], brow, 0.0), axis=1, keepdims=True)
                dec = jnp.exp(jnp.where(mk["incl"][d], gc_col - gc_row, neg_inf))
                a = jnp.where(mk["strict"][d], kk * bcol * dec, 0.0)
                a_list.append(a[:c] + a[c:])
                gl = (gc_row[:, c - 1:c], gc_row[:, n - 1:n]) if d == 0 else (gc_row[:, 0:1], gc_row[:, c:c + 1])
                gl_col = jnp.where(mk["top"], gl[0], gl[1])
                e1 = jnp.exp(gc_col)
                e2 = jnp.exp(gl_col - gc_col)
                vf = v_ref[0, rs, (2 * pp + h) * dv:(2 * pp + h + 1) * dv].astype(F32)
                s[h] = dict(rhs=jnp.concatenate([vf * bcol, kf * (bcol * e1)], axis=1).astype(BF16),
                            qe=(qf * e1).astype(BF16), e2=jnp.broadcast_to(e2, (n, dv)),
                            qkd=(qk * dec).astype(BF16), eg=[jnp.exp(gl[0]), jnp.exp(gl[1])])
            s["x"] = -jnp.concatenate(a_list, axis=1)
            s["q"] = mk["eye4"] + s["x"]

    def stage_t_first():
        for u in units:
            s = st[u]
            xb = s["x"].astype(BF16)
            s["x"] = _bdot(xb, _block_diag4(xb, mk))

    def stage_t_mid():
        for u in units:
            s = st[u]
            xb = s["x"].astype(BF16)
            y = _bdot(jnp.concatenate([s["q"].astype(BF16), xb], axis=0), _block_diag4(xb, mk))
            s["q"] = s["q"] + y[:c]
            s["x"] = y[c:]

    def stage_t_last():
        for u in units:
            s = st[u]
            s["q"] = s["q"] + _bdot(s["q"].astype(BF16), _block_diag4(s["x"].astype(BF16), mk))

    def stage_uw():
        for u in units:
            s = st[u]
            for h in range(2):
                th = s["q"][:, h * n:(h + 1) * n].astype(BF16)
                zero = jnp.zeros((), BF16)
                t_bd = jnp.concatenate([jnp.where(mk["left"], th, zero), jnp.where(mk["left"], zero, th)], axis=0)
                uw = _bdot(t_bd, s[h]["rhs"])
                s[h]["u"] = uw[:, :dv]
                s[h]["w"] = uw[:, dv:].astype(BF16)

    stages = [stage_kq, stage_t_first] + [stage_t_mid] * (CHUNK_SHIFT - 2) + [stage_t_last, stage_uw]
    return stages, st


def _delta_store_local(st, slot, u_s, w_s, qe_s, e2_s, qkd_s, eg_s):
    for pp in range(DELTA_PAIRS):
        for d in range(2):
            ch = slot * (2 * DELTA_PAIRS) + 2 * pp + d
            blocks = [st[(pp, d, b)] for b in range(DELTA_GROUP // 2)]
            for name, ref in (("u", u_s), ("w", w_s), ("qe", qe_s), ("e2", e2_s), ("qkd", qkd_s)):
                ref[ch] = jnp.concatenate(
                    [jnp.concatenate([blk[0][name], blk[1][name]], axis=1) for blk in blocks], axis=0)
            for b, blk in enumerate(blocks):
                for half in range(2):
                    j = 2 * b + half
                    eg_s[ch, j:j + 1, :] = jnp.concatenate(
                        [jnp.broadcast_to(blk[h]["eg"][half], (1, DN_DV)) for h in range(2)], axis=1)


def _delta_scan(k_ref, slot, u_s, w_s, qe_s, e2_s, qkd_s, eg_s, s_ref, o_ref, acc_ref, bases):
    c, dv = DN_CHUNK, DN_DV
    n_chains = 2 * DELTA_PAIRS
    first = slot * n_chains
    stages = []
    for jf in range(DELTA_GROUP):
        wq = {}

        def chunk_of(ch, jf=jf):
            return jf if ch % 2 == 0 else DELTA_GROUP - 1 - jf

        def stage1(wq=wq, chunk_of=chunk_of):
            for ch in range(n_chains):
                j = chunk_of(ch)
                rl = slice(j * c, (j + 1) * c)
                lhs = jnp.concatenate([w_s[first + ch, rl, :], qe_s[first + ch, rl, :]], axis=0)
                wq[ch] = _bdot(lhs, _block_diag2(s_ref[ch].astype(BF16)))

        def stage2(wq=wq, chunk_of=chunk_of):
            zpad = jnp.zeros((c, dv), BF16)
            for ch in range(n_chains):
                pp, d = ch // 2, ch % 2
                j = chunk_of(ch)
                rl = slice(j * c, (j + 1) * c)
                r0 = pl.multiple_of(bases[d] * DELTA_BLK + j * c, c)
                v_new = u_s[first + ch, rl, :] - wq[ch][:c]
                vb = v_new.astype(BF16)
                pads = []
                for h in range(2):
                    vh = vb[:, h * dv:(h + 1) * dv]
                    pads.append(jnp.concatenate([vh, zpad] if j % 2 == 0 else [zpad, vh], axis=0))
                v_bd = _block_diag2(jnp.concatenate(pads, axis=1))
                o = wq[ch][c:] + _bdot(qkd_s[first + ch, rl, :], v_bd)
                kc = k_ref[0, pl.ds(r0, c), pp * DN_DK:(pp + 1) * DN_DK]
                decay = eg_s[first + ch, j:j + 1, :]
                s_ref[ch] = s_ref[ch] * decay + _dot_tn(kc, (v_new * e2_s[first + ch, rl, :]).astype(BF16))
                if d == 0:
                    o_ref[0, pl.ds(r0, c), pp * PAIR_W:(pp + 1) * PAIR_W] = o.astype(o_ref.dtype)
                else:
                    acc_ref[pp, pl.ds(r0, c), :] = o.astype(acc_ref.dtype)

        stages += [stage1, stage2]
    return stages


def _delta_kernel(q_ref, k_ref, v_ref, ba_ref, alog_ref, dtb_ref, o_ref,
                  bg_ref, s_ref, acc_ref, u_s, w_s, qe_s, e2_s, qkd_s, eg_s):
    seq = q_ref.shape[1]
    n_groups = seq // DELTA_ROWS
    n = DELTA_BLK

    raw = ba_ref[0, 0]
    xx = raw + dtb_ref[0]
    softplus = jnp.maximum(xx, 0.0) + jnp.log(1.0 + jnp.exp(-jnp.abs(xx)))
    g = -jnp.exp(alog_ref[0]) * softplus
    is_beta = jnp.bitwise_and(lax.broadcasted_iota(jnp.int32, (raw.shape[0], 1), 0), 7) < 4
    bg = jnp.where(is_beta, jax.nn.sigmoid(raw), g)
    for i in range(seq // n):
        bg_ref[i] = bg[:, i * n:(i + 1) * n]
    s_ref[...] = jnp.zeros(s_ref.shape, F32)

    mk = _delta_masks()
    local_scratch = (u_s, w_s, qe_s, e2_s, qkd_s, eg_s)

    def group_bases(t):
        return (t * (DELTA_GROUP // 2), (n_groups - 1 - t) * (DELTA_GROUP // 2))

    stages, st = _delta_local(q_ref, k_ref, v_ref, bg_ref, group_bases(0), mk)
    for stage in stages:
        stage()
    _delta_store_local(st, 0, *local_scratch)

    def trip(t, carry):
        nxt = jnp.minimum(t + 1, n_groups - 1)
        slot = jnp.bitwise_and(t, 1)
        l_stages, l_st = _delta_local(q_ref, k_ref, v_ref, bg_ref, group_bases(nxt), mk)
        s_stages = _delta_scan(k_ref, slot, *local_scratch, s_ref, o_ref, acc_ref, group_bases(t))
        for a, b in zip(l_stages, s_stages):
            a()
            b()
        _delta_store_local(l_st, 1 - slot, *local_scratch)
        return carry

    lax.fori_loop(0, n_groups, trip, 0)

    rows = min(512, seq)
    for r in range(0, seq, rows):
        for pp in range(DELTA_PAIRS):
            cols = slice(pp * PAIR_W, (pp + 1) * PAIR_W)
            o = o_ref[0, r:r + rows, cols].astype(F32) + acc_ref[pp, r:r + rows, :].astype(F32)
            o_ref[0, r:r + rows, cols] = o.astype(o_ref.dtype)


def _dn_delta(qkv_n, ba_r, alog_r, dtb_r):
    b, seq, _ = qkv_n.shape
    assert seq % DELTA_ROWS == 0 and len(_delta_local(None, None, None, None, None, None)[0]) == 2 * DELTA_GROUP
    n_steps = DN_NK // DELTA_PAIRS
    qk_w = DELTA_PAIRS * DN_DK
    v_w = DELTA_PAIRS * PAIR_W
    k_tile0 = DN_K_DIM // qk_w
    v_tile0 = 2 * DN_K_DIM // v_w
    n_rows = 8 * DELTA_PAIRS
    n_chains = 2 * DELTA_PAIRS
    grp = lambda dt: pltpu.VMEM((2 * n_chains, DELTA_ROWS, PAIR_W), dt)
    block_bytes = 2 * (2 * seq * qk_w * 2 + 2 * seq * v_w * 2 + n_rows * seq * 4)
    scratch_bytes = (DELTA_PAIRS * seq * PAIR_W * 2 + 2 * n_chains * DELTA_ROWS * PAIR_W * (4 + 2 + 2 + 4 + 2)
                     + n_chains * DN_DK * PAIR_W * 4 + n_rows * seq * 4 + 2 * n_chains * 8 * PAIR_W * 4)
    vmem = block_bytes + scratch_bytes + DELTA_SPILL_BYTES
    return pl.pallas_call(
        _delta_kernel,
        grid=(b, n_steps),
        in_specs=[pl.BlockSpec((1, seq, qk_w), lambda i, p: (i, 0, p)),
                  pl.BlockSpec((1, seq, qk_w), lambda i, p: (i, 0, k_tile0 + p)),
                  pl.BlockSpec((1, seq, v_w), lambda i, p: (i, 0, v_tile0 + p)),
                  pl.BlockSpec((1, 1, n_rows, seq), lambda i, p: (i, p, 0, 0)),
                  pl.BlockSpec((1, n_rows, 1), lambda i, p: (p, 0, 0)),
                  pl.BlockSpec((1, n_rows, 1), lambda i, p: (p, 0, 0))],
        out_specs=pl.BlockSpec((1, seq, v_w), lambda i, p: (i, 0, p)),
        out_shape=jax.ShapeDtypeStruct((b, seq, DN_V_DIM), BF16),
        scratch_shapes=[pltpu.VMEM((seq // DELTA_BLK, n_rows, DELTA_BLK), F32),
                        pltpu.VMEM((n_chains, DN_DK, PAIR_W), F32),
                        pltpu.VMEM((DELTA_PAIRS, seq, PAIR_W), BF16),
                        grp(F32), grp(BF16), grp(BF16), grp(F32), grp(BF16),
                        pltpu.VMEM((2 * n_chains, 8, PAIR_W), F32)],
        compiler_params=_cparams("parallel", "parallel", vmem=vmem),
        name="dn_delta",
    )(qkv_n, qkv_n, qkv_n, ba_r, alog_r, dtb_r)


def _step_rows(t):
    n_steps = DN_NK // DELTA_PAIRS
    t = t.astype(F32).reshape(2, n_steps, DELTA_PAIRS, 2).transpose(1, 2, 0, 3).reshape(n_steps, DELTA_PAIRS, 4)
    return jnp.concatenate([jnp.zeros_like(t), t], axis=2).reshape(n_steps, 8 * DELTA_PAIRS, 1)


def _gated_proj_ln_kernel(o_ref, z_ref, nw_ref, w_ref, x_ref, g_ref, b_ref, out_ref):
    nw = nw_ref[...]
    cols = []
    for h in range(DN_NV):
        sl = slice(h * DN_DV, (h + 1) * DN_DV)
        o = o_ref[:, sl].astype(F32)
        ms = jnp.mean(o * o, axis=-1, keepdims=True)
        cols.append((o * lax.rsqrt(ms + RMS_EPS) * nw * _silu(z_ref[:, sl].astype(F32))).astype(BF16))
    y = _bdot(jnp.concatenate(cols, axis=1), w_ref[...])
    out_ref[...] = _layernorm(DEEPNORM_ALPHA * x_ref[...] + y, g_ref[...], b_ref[...])


def _gated_proj_ln(o, qkvz, norm_w, w, x, g, b, tm=512):
    t, k = o.shape
    d = w.shape[1]
    tm = min(tm, t)
    z_tile = DN_CONV_CH // k
    return pl.pallas_call(
        _gated_proj_ln_kernel,
        grid=(t // tm,),
        in_specs=[pl.BlockSpec((tm, k), lambda i: (i, 0)),
                  pl.BlockSpec((tm, k), lambda i: (i, z_tile)),
                  pl.BlockSpec((1, DN_DV), lambda i: (0, 0)),
                  pl.BlockSpec((k, d), lambda i: (0, 0)),
                  pl.BlockSpec((tm, d), lambda i: (i, 0)),
                  pl.BlockSpec((1, d), lambda i: (0, 0)),
                  pl.BlockSpec((1, d), lambda i: (0, 0))],
        out_specs=pl.BlockSpec((tm, d), lambda i: (i, 0)),
        out_shape=jax.ShapeDtypeStruct((t, d), F32),
        compiler_params=_cparams("parallel"),
        name="gated_proj_ln",
    )(o, qkvz, norm_w.reshape(1, DN_DV), w, x, g.reshape(1, d), b.reshape(1, d))


def _deltanet_layer(xf, bsz, seq, w_in, conv_w, a_log, dt_bias, norm_w, w_out, ln_g, ln_b):
    t = xf.shape[0]
    n_qkvz = DN_CONV_CH + DN_V_DIM
    n_steps = DN_NK // DELTA_PAIRS
    w_bf = w_in.astype(BF16)
    qkvz = _matmul(xf, w_bf[:, :n_qkvz], BF16, tm=1024, tn=1536)
    ba = _matmul(xf, w_bf[:, n_qkvz:], F32, tm=1024, tn=4 * DN_NV)
    ba_r = ba.reshape(bsz, seq, 2, 2, n_steps, DELTA_PAIRS, 2).transpose(0, 4, 5, 2, 3, 6, 1)
    ba_r = ba_r.reshape(bsz, n_steps, 8 * DELTA_PAIRS, seq)
    qkv_n = _dn_conv(qkvz.reshape(bsz, seq, n_qkvz), conv_w)
    o = _dn_delta(qkv_n, ba_r, _step_rows(a_log), _step_rows(dt_bias))
    return _gated_proj_ln(o.reshape(t, DN_V_DIM), qkvz, norm_w, w_out.astype(BF16), xf, ln_g, ln_b)


MLA_QK = MLA_NOPE + MLA_ROPE
MLA_IN_EXT = MLA_QR + MLA_KVR + 2 * LANES
MLA_Q_EXT = MLA_NOPE + 2 * MLA_ROPE
MLA_V_EXT = MLA_V + 16
LOG2E = 1.4426950408889634


def _rope_kernel(pos_ref, invf_ref, sign_ref, cs_ref, sn_ref):
    ang = invf_ref[...] * pos_ref[0]
    cs_ref[0] = jnp.cos(ang)
    sn_ref[0] = jnp.sin(ang) * sign_ref[...]


def _rope_tables(positions):
    b, seq = positions.shape
    half = MLA_ROPE // 2
    inv_freq = 1.0 / (MLA_ROPE_THETA ** (jnp.arange(0, MLA_ROPE, 2, dtype=F32) / MLA_ROPE))
    invf = jnp.concatenate([inv_freq, inv_freq]).reshape(MLA_ROPE, 1)
    sign = jnp.concatenate([-jnp.ones((half,), F32), jnp.ones((half,), F32)]).reshape(MLA_ROPE, 1)
    out = jax.ShapeDtypeStruct((b, MLA_ROPE, seq), F32)
    return pl.pallas_call(
        _rope_kernel,
        grid=(b,),
        in_specs=[pl.BlockSpec((1, 1, seq), lambda i: (i, 0, 0)),
                  pl.BlockSpec((MLA_ROPE, 1), lambda i: (0, 0)),
                  pl.BlockSpec((MLA_ROPE, 1), lambda i: (0, 0))],
        out_specs=[pl.BlockSpec((1, MLA_ROPE, seq), lambda i: (i, 0, 0)),
                   pl.BlockSpec((1, MLA_ROPE, seq), lambda i: (i, 0, 0))],
        out_shape=[out, out],
        compiler_params=_cparams("parallel"),
        name="rope_tables",
    )(positions.astype(F32).reshape(b, 1, seq), invf, sign)


def _rms(x, w):
    return x * lax.rsqrt(jnp.mean(x * x, axis=-1, keepdims=True) + RMS_EPS) * w


def _mla_in_kernel(x_ref, w_ref, qn_ref, kvn_ref, cs_ref, sn_ref, cq_ref, ckv_ref, kr_ref):
    h = _bdot(x_ref[...].astype(BF16), w_ref[...])
    cq_ref[...] = _rms(h[:, :MLA_QR], qn_ref[...]).astype(cq_ref.dtype)
    ckv_ref[...] = _rms(h[:, MLA_QR:MLA_QR + MLA_KVR], kvn_ref[...]).astype(ckv_ref.dtype)
    r0 = MLA_QR + MLA_KVR
    kr = h[:, r0:r0 + MLA_ROPE] * cs_ref[...] + h[:, r0 + LANES:r0 + LANES + MLA_ROPE] * sn_ref[...]
    kr_ref[...] = kr.astype(kr_ref.dtype)


def _mla_in(xf, w_ext, q_norm, kv_norm, cs, sn, tm=512):
    t, d = xf.shape
    tm = min(tm, t)
    row = lambda i: (i, 0)
    fix = lambda i: (0, 0)
    return pl.pallas_call(
        _mla_in_kernel,
        grid=(t // tm,),
        in_specs=[pl.BlockSpec((tm, d), row),
                  pl.BlockSpec((d, MLA_IN_EXT), fix),
                  pl.BlockSpec((1, MLA_QR), fix),
                  pl.BlockSpec((1, MLA_KVR), fix),
                  pl.BlockSpec((tm, MLA_ROPE), row),
                  pl.BlockSpec((tm, MLA_ROPE), row)],
        out_specs=[pl.BlockSpec((tm, MLA_QR), row),
                   pl.BlockSpec((tm, MLA_KVR), row),
                   pl.BlockSpec((tm, MLA_ROPE), row)],
        out_shape=[jax.ShapeDtypeStruct((t, MLA_QR), BF16),
                   jax.ShapeDtypeStruct((t, MLA_KVR), BF16),
                   jax.ShapeDtypeStruct((t, MLA_ROPE), BF16)],
        compiler_params=_cparams("parallel"),
        name="mla_in",
    )(xf, w_ext, q_norm.reshape(1, MLA_QR), kv_norm.reshape(1, MLA_KVR), cs, sn)


def _mla_q_kernel(cq_ref, w_ref, cs_ref, sn_ref, q_ref):
    acc = _dot_nt(w_ref[...], cq_ref[...])
    cs = cs_ref[0]
    sn = sn_ref[0]
    scale = MLA_QK ** -0.5 * LOG2E
    for h in range(MLA_H):
        a0 = h * MLA_Q_EXT
        o0 = h * MLA_QK
        q_ref[0, o0:o0 + MLA_NOPE, :] = (acc[a0:a0 + MLA_NOPE] * scale).astype(q_ref.dtype)
        x = acc[a0 + MLA_NOPE:a0 + MLA_NOPE + MLA_ROPE]
        xs = acc[a0 + MLA_NOPE + MLA_ROPE:a0 + MLA_Q_EXT]
        q_ref[0, o0 + MLA_NOPE:o0 + MLA_QK, :] = ((x * cs + xs * sn) * scale).astype(q_ref.dtype)


def _mla_q(cq, wq_t, cs_t, sn_t, bsz, seq, tl=512):
    tl = min(tl, seq)
    nl = seq // tl
    return pl.pallas_call(
        _mla_q_kernel,
        grid=(bsz, nl),
        in_specs=[pl.BlockSpec((tl, MLA_QR), lambda b, i: (b * nl + i, 0)),
                  pl.BlockSpec((MLA_H * MLA_Q_EXT, MLA_QR), lambda b, i: (0, 0)),
                  pl.BlockSpec((1, MLA_ROPE, tl), lambda b, i: (b, 0, i)),
                  pl.BlockSpec((1, MLA_ROPE, tl), lambda b, i: (b, 0, i))],
        out_specs=pl.BlockSpec((1, MLA_H * MLA_QK, tl), lambda b, i: (b, 0, i)),
        out_shape=jax.ShapeDtypeStruct((bsz, MLA_H * MLA_QK, seq), BF16),
        compiler_params=_cparams("parallel", "parallel"),
        name="mla_q",
    )(cq, wq_t, cs_t, sn_t)


def _mla_kv_kernel(ckv_ref, kr_ref, wk_ref, wv_ref, k_ref, v_ref):
    ckv = ckv_ref[...]
    kn = _bdot(ckv, wk_ref[...])
    kr = kr_ref[...]
    for h in range(MLA_H):
        k_ref[0, h] = jnp.concatenate(
            [kn[:, h * MLA_NOPE:(h + 1) * MLA_NOPE].astype(k_ref.dtype), kr], axis=1)
    v_t = _dot_nt(wv_ref[...], ckv).astype(v_ref.dtype)
    ones = jnp.ones((MLA_V_EXT - MLA_V, v_t.shape[1]), v_ref.dtype)
    for h in range(MLA_H):
        v_ref[0, h * MLA_V_EXT:h * MLA_V_EXT + MLA_V, :] = v_t[h * MLA_V:(h + 1) * MLA_V]
        v_ref[0, h * MLA_V_EXT + MLA_V:(h + 1) * MLA_V_EXT, :] = ones


def _mla_kv(ckv, kr, wk, wv_t, bsz, seq, tl=512):
    tl = min(tl, seq)
    nl = seq // tl
    return pl.pallas_call(
        _mla_kv_kernel,
        grid=(bsz, nl),
        in_specs=[pl.BlockSpec((tl, MLA_KVR), lambda b, i: (b * nl + i, 0)),
                  pl.BlockSpec((tl, MLA_ROPE), lambda b, i: (b * nl + i, 0)),
                  pl.BlockSpec((MLA_KVR, MLA_H * MLA_NOPE), lambda b, i: (0, 0)),
                  pl.BlockSpec((MLA_H * MLA_V, MLA_KVR), lambda b, i: (0, 0))],
        out_specs=[pl.BlockSpec((1, MLA_H, tl, MLA_QK), lambda b, i: (b, 0, i, 0)),
                   pl.BlockSpec((1, MLA_H * MLA_V_EXT, tl), lambda b, i: (b, 0, i))],
        out_shape=[jax.ShapeDtypeStruct((bsz, MLA_H, seq, MLA_QK), BF16),
                   jax.ShapeDtypeStruct((bsz, MLA_H * MLA_V_EXT, seq), BF16)],
        compiler_params=_cparams("parallel", "parallel"),
        name="mla_kv",
    )(ckv, kr, wk, wv_t)


ATTN_TQ = 1024
ATTN_TK = 512


def _attn_kernel(q_ref, k_ref, v_ref, o_ref):
    q_t = q_ref[0]
    tq = q_t.shape[1]
    seq = k_ref.shape[2]
    tk = min(ATTN_TK, seq)
    m = jnp.full((1, tq), -jnp.inf, F32)
    acc = jnp.zeros((MLA_V_EXT, tq), F32)
    n_k = seq // tk
    scores = lambda j: _bdot(k_ref[0, 0, j * tk:(j + 1) * tk, :], q_t)
    s_next = scores(0)
    for j in range(n_k):
        s = s_next
        if j + 1 < n_k:
            s_next = scores(j + 1)
        m_new = jnp.maximum(m, jnp.max(s, axis=0, keepdims=True))
        p = jnp.exp2(s - m_new).astype(BF16)
        acc = jnp.exp2(m - m_new) * acc + _bdot(v_ref[0, :, j * tk:(j + 1) * tk], p)
        m = m_new
    o_ref[0] = (acc[:MLA_V] / acc[MLA_V:MLA_V + 1]).T.astype(o_ref.dtype)


def _mla_attn(q_t, k, v_t, bsz, seq):
    tq = min(ATTN_TQ, seq)
    return pl.pallas_call(
        _attn_kernel,
        grid=(bsz, MLA_H, seq // tq),
        in_specs=[pl.BlockSpec((1, MLA_QK, tq), lambda b, h, i: (b, h, i)),
                  pl.BlockSpec((1, 1, seq, MLA_QK), lambda b, h, i: (b, h, 0, 0)),
                  pl.BlockSpec((1, MLA_V_EXT, seq), lambda b, h, i: (b, h, 0))],
        out_specs=pl.BlockSpec((1, tq, MLA_V), lambda b, h, i: (b, i, h)),
        out_shape=jax.ShapeDtypeStruct((bsz, seq, MLA_H * MLA_V), BF16),
        compiler_params=_cparams("parallel", "parallel", "parallel"),
        name="mla_attn",
    )(q_t, k, v_t)


def _swap_halves(t):
    half = t.shape[-1] // 2
    return jnp.concatenate([t[..., half:], t[..., :half]], axis=-1)


def _mla_layer(xf, bsz, seq, cs, sn, cs_t, sn_t, w_in, q_norm, w_qb, kv_norm, w_kvb, w_out, ln_g, ln_b):
    t, d = xf.shape
    r0 = MLA_QR + MLA_KVR
    w_rope = w_in[:, r0:]
    pad = jnp.zeros((d, LANES - MLA_ROPE), w_in.dtype)
    w_ext = jnp.concatenate([w_in, pad, _swap_halves(w_rope), pad], axis=1).astype(BF16)
    wq = w_qb.reshape(MLA_QR, MLA_H, MLA_QK)
    wq_rope = wq[:, :, MLA_NOPE:]
    wq_t = jnp.concatenate([wq, _swap_halves(wq_rope)], axis=-1).reshape(MLA_QR, MLA_H * MLA_Q_EXT).T.astype(BF16)
    wkv = w_kvb.reshape(MLA_KVR, MLA_H, MLA_NOPE + MLA_V)
    wk = wkv[:, :, :MLA_NOPE].reshape(MLA_KVR, MLA_H * MLA_NOPE).astype(BF16)
    wv_t = wkv[:, :, MLA_NOPE:].reshape(MLA_KVR, MLA_H * MLA_V).T.astype(BF16)

    cq, ckv, kr = _mla_in(xf, w_ext, q_norm, kv_norm, cs, sn)
    q_t = _mla_q(cq, wq_t, cs_t, sn_t, bsz, seq)
    k, v_t = _mla_kv(ckv, kr, wk, wv_t, bsz, seq)
    o = _mla_attn(q_t, k, v_t, bsz, seq)
    return _proj_ln(o.reshape(t, MLA_H * MLA_V), w_out.astype(BF16), xf, ln_g, ln_b)


def _ind(mask):
    return jnp.where(mask, 1.0, 0.0)


def _route_rows(logits_t, bias_col):
    scores = jax.nn.sigmoid(logits_t)
    sel = scores + bias_col
    s = [sel[e:e + 1] for e in range(N_EXPERTS)]
    sc = [scores[e:e + 1] for e in range(N_EXPERTS)]

    def ranks(vals):
        rank = [jnp.zeros_like(vals[0]) for _ in vals]
        for i in range(len(vals)):
            for k in range(i + 1, len(vals)):
                k_wins = _ind(vals[k] > vals[i])
                rank[i] = rank[i] + k_wins
                rank[k] = rank[k] + (1.0 - k_wins)
        return rank

    top2, gsum = [], []
    for g in range(N_GROUPS):
        idx = range(g * EXPERTS_PER_GROUP, (g + 1) * EXPERTS_PER_GROUP)
        rk = ranks([s[e] for e in idx])
        t2 = [r < 2.0 for r in rk]
        top2.append(t2)
        gsum.append(sum(jnp.where(t, s[e], 0.0) for t, e in zip(t2, idx)))
    won = [r < 0.5 for r in ranks(gsum)]
    w = [sum(jnp.where(won[g], jnp.where(top2[g][e], sc[g * EXPERTS_PER_GROUP + e], 0.0), 0.0)
             for g in range(N_GROUPS)) for e in range(EXPERTS_PER_GROUP)]
    tot = sum(w)
    return [we / tot for we in w], [_ind(m) for m in won]


MOE_BLOCK = 128
GATE_ROWS = 8


def _router_weights(router_w):
    hi, lo = _split_bf16(router_w.astype(F32).T)
    return jnp.concatenate([hi, lo], axis=0)


def _split_bf16(v):
    hi = v.astype(BF16)
    return hi, (v - hi.astype(F32)).astype(BF16)


def _rows8(rows):
    pad = [jnp.zeros_like(rows[0])] * (GATE_ROWS - len(rows))
    return jnp.concatenate(list(rows) + pad, axis=0)


def _moe_kernel(x_ref, rw_ref, rb_ref, wg_ref, wu_ref, wd_ref, g_ref, b_ref, o_ref,
                xs_ref, pt_ref, gate_ref, acc_ref, seg_ref):
    j = pl.program_id(1)
    tm = x_ref.shape[0]

    @pl.when(j == 0)
    def _():
        x = x_ref[...]
        xh, xl = _split_bf16(x)
        rw = rw_ref[...]
        lg = _dot_nt(rw, xh)
        logits_t = lg[:N_EXPERTS] + lg[N_EXPERTS:] + _dot_nt(rw[:N_EXPERTS], xl)
        gates, member = _route_rows(logits_t, rb_ref[...])
        ri = lax.broadcasted_iota(jnp.int32, (tm, tm), 0)
        ci = lax.broadcasted_iota(jnp.int32, (tm, tm), 1)
        utri = jnp.where(ri <= ci, 1.0, 0.0).astype(BF16)
        prefix = _bdot(_rows8(member).astype(BF16), utri)
        cnt = [prefix[g:g + 1, tm - 1:tm] for g in range(N_GROUPS)]
        off = [jnp.zeros((1, 1), F32)]
        for g in range(1, N_GROUPS):
            off.append(off[-1] + cnt[g - 1])
        pos_row = sum(member[g] * (off[g] + prefix[g:g + 1] - 1.0) for g in range(N_GROUPS))
        pos_col = jnp.sum(jnp.where(ri == ci, pos_row, 0.0), axis=1, keepdims=True)
        p = jnp.where(pos_row == ri.astype(F32), 1.0, 0.0).astype(BF16)
        pt_ref[...] = jnp.where(pos_col == ci.astype(F32), 1.0, 0.0).astype(BF16)
        xs_ref[...] = _bdot(p, xh).astype(BF16)
        gh, gl = _split_bf16(_rows8(gates))
        gate_ref[...] = _dot_nt(p, gh) + _dot_nt(p, gl)
        for g in range(N_GROUPS):
            seg_ref[g] = off[g][0, 0].astype(jnp.int32)
            seg_ref[N_GROUPS + g] = cnt[g][0, 0].astype(jnp.int32)
        acc_ref[...] = jnp.zeros(acc_ref.shape, F32)

    seg_lo = seg_ref[j]
    seg_hi = seg_lo + seg_ref[N_GROUPS + j]
    for r in range(0, tm, MOE_BLOCK):
        @pl.when(jnp.logical_and(seg_lo < r + MOE_BLOCK, seg_hi > r))
        def _(r=r):
            xb = xs_ref[r:r + MOE_BLOCK, :]
            row = r + lax.broadcasted_iota(jnp.int32, (MOE_BLOCK, 1), 0)
            mine = jnp.logical_and(row >= seg_lo, row < seg_hi)
            gates = jnp.where(mine, gate_ref[r:r + MOE_BLOCK, :], 0.0)
            hid = []
            for e in range(EXPERTS_PER_GROUP):
                hg = _bdot(xb, wg_ref[e])
                hu = _bdot(xb, wu_ref[e])
                hid.append((_silu(hg) * hu * gates[:, e:e + 1]).astype(BF16))
            acc_ref[r:r + MOE_BLOCK, :] += _bdot(jnp.concatenate(hid, axis=1), wd_ref[...])

    @pl.when(j == N_GROUPS - 1)
    def _():
        y = _bdot(pt_ref[...], acc_ref[...].astype(BF16))
        o_ref[...] = _layernorm(DEEPNORM_ALPHA * x_ref[...] + y, g_ref[...], b_ref[...])


def _moe_layer(xf, rw_t, rb_t, w_gate, w_up, w_down, ln_g, ln_b, tm=512):
    t, d = xf.shape
    tm = min(tm, t)
    f = D_FF_EXPERT
    gf = EXPERTS_PER_GROUP * f
    row = lambda i, j: (i, 0)
    fix = lambda i, j: (0, 0)
    return pl.pallas_call(
        _moe_kernel,
        grid=(t // tm, N_GROUPS),
        in_specs=[pl.BlockSpec((tm, d), row),
                  pl.BlockSpec((2 * N_EXPERTS, d), fix),
                  pl.BlockSpec((N_EXPERTS, 1), fix),
                  pl.BlockSpec((EXPERTS_PER_GROUP, d, f), lambda i, j: (j, 0, 0)),
                  pl.BlockSpec((EXPERTS_PER_GROUP, d, f), lambda i, j: (j, 0, 0)),
                  pl.BlockSpec((gf, d), lambda i, j: (j, 0)),
                  pl.BlockSpec((1, d), fix),
                  pl.BlockSpec((1, d), fix)],
        out_specs=pl.BlockSpec((tm, d), row),
        out_shape=jax.ShapeDtypeStruct((t, d), F32),
        scratch_shapes=[pltpu.VMEM((tm, d), BF16),
                        pltpu.VMEM((tm, tm), BF16),
                        pltpu.VMEM((tm, GATE_ROWS), F32),
                        pltpu.VMEM((tm, d), F32),
                        pltpu.SMEM((2 * N_GROUPS,), jnp.int32)],
        compiler_params=_cparams("parallel", "arbitrary"),
        name="moe",
    )(xf, rw_t, rb_t, w_gate.astype(BF16), w_up.astype(BF16),
      w_down.astype(BF16).reshape(N_EXPERTS * f, d), ln_g.reshape(1, d), ln_b.reshape(1, d))


def kernel(x, positions, dn_w_in, dn_conv_w, dn_a_log, dn_dt_bias, dn_norm_w, dn_w_out, mla_w_in, mla_q_norm, mla_w_qb, mla_kv_norm, mla_w_kvb, mla_w_out, ln1_g, ln1_b, ln2_g, ln2_b, router_w, router_bias, moe_w_gate, moe_w_up, moe_w_down):
    bsz, seq, d = x.shape
    t = bsz * seq
    xf = x.reshape(t, d)
    cs_t, sn_t = _rope_tables(positions)
    cs = cs_t.transpose(0, 2, 1).reshape(t, MLA_ROPE)
    sn = sn_t.transpose(0, 2, 1).reshape(t, MLA_ROPE)
    rw_t = _router_weights(router_w)
    rb_t = router_bias.astype(F32).reshape(N_EXPERTS, 1)
    for i in range(DEPTH):
        j = i // 2
        if i % 2 == 0:
            xf = _deltanet_layer(xf, bsz, seq, dn_w_in[j], dn_conv_w[j], dn_a_log[j], dn_dt_bias[j],
                                 dn_norm_w[j], dn_w_out[j], ln1_g[i], ln1_b[i])
        else:
            xf = _mla_layer(xf, bsz, seq, cs, sn, cs_t, sn_t, mla_w_in[j], mla_q_norm[j], mla_w_qb[j],
                            mla_kv_norm[j], mla_w_kvb[j], mla_w_out[j], ln1_g[i], ln1_b[i])
        xf = _moe_layer(xf, rw_t, rb_t, moe_w_gate[i], moe_w_up[i], moe_w_down[i], ln2_g[i], ln2_b[i])
    return xf.reshape(bsz, seq, d)
```

```python
import functools

import jax
import jax.numpy as jnp
from jax import lax
from jax.experimental import pallas as pl
from jax.experimental.pallas import tpu as pltpu

F32 = jnp.float32
BF16 = jnp.bfloat16

DEPTH = 4
DN_NK = 8
DN_NV = 16
DN_DK = 128
DN_DV = 128
DN_K_DIM = DN_NK * DN_DK
DN_V_DIM = DN_NV * DN_DV
DN_CONV_CH = 2 * DN_K_DIM + DN_V_DIM
DN_CONV_W = 5
DN_CHUNK = 64
MLA_H = 8
MLA_NOPE = 128
MLA_ROPE = 64
MLA_V = 128
MLA_QR = 384
MLA_KVR = 256
MLA_ROPE_THETA = 10000.0
N_EXPERTS = 16
N_GROUPS = 4
EXPERTS_PER_GROUP = N_EXPERTS // N_GROUPS
D_FF_EXPERT = 256
DEEPNORM_ALPHA = (2 * DEPTH) ** 0.25
LN_EPS = 1e-5
RMS_EPS = 1e-6

LANES = 128
VMEM_LIMIT = 48 * 1024 * 1024


def _cparams(*sems, vmem=VMEM_LIMIT):
    return pltpu.CompilerParams(dimension_semantics=sems, vmem_limit_bytes=vmem)


def _bdot(a, b):
    return jnp.dot(a, b, preferred_element_type=F32)


def _dot_nt(a, b):
    return lax.dot_general(a, b, (((1,), (1,)), ((), ())), preferred_element_type=F32)


def _dot_tn(a, b):
    return lax.dot_general(a, b, (((0,), (0,)), ((), ())), preferred_element_type=F32)


def _silu(x):
    return x * jax.nn.sigmoid(x)


def _mm_kernel(x_ref, w_ref, o_ref):
    o_ref[...] = _bdot(x_ref[...].astype(BF16), w_ref[...]).astype(o_ref.dtype)


def _matmul(x, w, out_dtype, tm, tn):
    t, k = x.shape
    n = w.shape[1]
    tm = min(tm, t)
    return pl.pallas_call(
        _mm_kernel,
        grid=(n // tn, t // tm),
        in_specs=[pl.BlockSpec((tm, k), lambda j, i: (i, 0)),
                  pl.BlockSpec((k, tn), lambda j, i: (0, j))],
        out_specs=pl.BlockSpec((tm, tn), lambda j, i: (i, j)),
        out_shape=jax.ShapeDtypeStruct((t, n), out_dtype),
        compiler_params=_cparams("parallel", "parallel"),
        name="proj",
    )(x, w)


def _layernorm(h, g, b):
    mu = jnp.mean(h, axis=-1, keepdims=True)
    hc = h - mu
    var = jnp.mean(hc * hc, axis=-1, keepdims=True)
    return hc * lax.rsqrt(var + LN_EPS) * g + b


def _proj_ln_kernel(a_ref, w_ref, x_ref, g_ref, b_ref, o_ref):
    y = _bdot(a_ref[...], w_ref[...])
    o_ref[...] = _layernorm(DEEPNORM_ALPHA * x_ref[...] + y, g_ref[...], b_ref[...])


def _proj_ln(a, w, x, g, b, tm=512):
    t, k = a.shape
    d = w.shape[1]
    tm = min(tm, t)
    return pl.pallas_call(
        _proj_ln_kernel,
        grid=(t // tm,),
        in_specs=[pl.BlockSpec((tm, k), lambda i: (i, 0)),
                  pl.BlockSpec((k, d), lambda i: (0, 0)),
                  pl.BlockSpec((tm, d), lambda i: (i, 0)),
                  pl.BlockSpec((1, d), lambda i: (0, 0)),
                  pl.BlockSpec((1, d), lambda i: (0, 0))],
        out_specs=pl.BlockSpec((tm, d), lambda i: (i, 0)),
        out_shape=jax.ShapeDtypeStruct((t, d), F32),
        compiler_params=_cparams("parallel"),
        name="proj_ln",
    )(a, w, x, g.reshape(1, d), b.reshape(1, d))


CONV_PAD_ROWS = 8
CONV_ROWS = 512


def _conv_kernel(x_ref, w_ref, o_ref, pad_ref):
    c = pl.program_id(1)
    seq = x_ref.shape[1]
    zeros = jnp.zeros((CONV_PAD_ROWS, LANES), F32)
    pad_ref[0:CONV_PAD_ROWS, :] = zeros
    pad_ref[seq + CONV_PAD_ROWS:seq + 2 * CONV_PAD_ROWS, :] = zeros
    pad_ref[CONV_PAD_ROWS:seq + CONV_PAD_ROWS, :] = x_ref[0].astype(F32)
    w = w_ref[...]
    is_q = c < DN_NK
    is_v = c >= 2 * DN_NK
    rows = min(CONV_ROWS, seq)
    half = DN_CONV_W // 2
    for r in range(0, seq, rows):
        acc = None
        for j in range(DN_CONV_W):
            lo = CONV_PAD_ROWS + r + j - half
            term = w[j:j + 1, :] * pad_ref[lo:lo + rows, :]
            acc = term if acc is None else acc + term
        y = _silu(acc)
        nrm = lax.rsqrt(jnp.sum(y * y, axis=-1, keepdims=True) + RMS_EPS)
        nrm = nrm * jnp.where(is_q, DN_DK ** -0.5, 1.0)
        scale = jnp.where(is_v, jnp.ones_like(nrm), nrm)
        o_ref[0, r:r + rows, :] = (y * scale).astype(o_ref.dtype)


def _dn_conv(qkvz, conv_w):
    b, seq, _ = qkvz.shape
    n_tiles = DN_CONV_CH // LANES
    return pl.pallas_call(
        _conv_kernel,
        grid=(b, n_tiles),
        in_specs=[pl.BlockSpec((1, seq, LANES), lambda i, c: (i, 0, c)),
                  pl.BlockSpec((DN_CONV_W, LANES), lambda i, c: (0, c))],
        out_specs=pl.BlockSpec((1, seq, LANES), lambda i, c: (i, 0, c)),
        out_shape=jax.ShapeDtypeStruct((b, seq, DN_CONV_CH), BF16),
        scratch_shapes=[pltpu.VMEM((seq + 2 * CONV_PAD_ROWS, LANES), F32)],
        compiler_params=_cparams("parallel", "parallel"),
        name="dn_conv",
    )(qkvz, conv_w)


DELTA_GROUP = 4
DELTA_PAIRS = 2
DELTA_BLK = 2 * DN_CHUNK
DELTA_ROWS = DELTA_GROUP * DN_CHUNK
CHUNK_SHIFT = DN_CHUNK.bit_length() - 1
PAIR_W = 2 * DN_DV
DELTA_SPILL_BYTES = 14 * 1024 * 1024


def _block_diag2(x):
    r = x.shape[0]
    z = jnp.zeros((r, r), x.dtype)
    return jnp.concatenate([jnp.concatenate([x[:, :r], z], axis=1),
                            jnp.concatenate([z, x[:, r:]], axis=1)], axis=0)


def _delta_masks():
    n, c = DELTA_BLK, DN_CHUNK
    ii = lax.broadcasted_iota(jnp.int32, (n, n), 0)
    jj = lax.broadcasted_iota(jnp.int32, (n, n), 1)
    same = jnp.right_shift(ii, CHUNK_SHIFT) == jnp.right_shift(jj, CHUNK_SHIFT)
    ge = jnp.logical_and(same, ii >= jj)
    gt = jnp.logical_and(same, ii > jj)
    le = jnp.logical_and(same, ii <= jj)
    lt = jnp.logical_and(same, ii < jj)
    top = lax.broadcasted_iota(jnp.int32, (n, 1), 0) < c
    r4 = lax.broadcasted_iota(jnp.int32, (c, 4 * c), 0)
    l4 = lax.broadcasted_iota(jnp.int32, (c, 4 * c), 1)
    eye4 = jnp.where(r4 == jnp.bitwise_and(l4, c - 1), 1.0, 0.0).astype(F32)
    rb = lax.broadcasted_iota(jnp.int32, (4 * c, 4 * c), 0)
    lb = lax.broadcasted_iota(jnp.int32, (4 * c, 4 * c), 1)
    bd4 = jnp.right_shift(rb, CHUNK_SHIFT) == jnp.right_shift(lb, CHUNK_SHIFT)
    left = lax.broadcasted_iota(jnp.int32, (c, n), 1) < c
    return dict(eye=ii == jj, eye4=eye4, bd4=bd4, left=left, top=top,
                incl=(ge, le), strict=(gt, lt), cum=(ge, le))


def _block_diag4(x, mk):
    return jnp.where(mk["bd4"], jnp.concatenate([x] * 4, axis=0), jnp.zeros((), x.dtype))


def _delta_local(q_ref, k_ref, v_ref, bg_ref, bases, mk):
    c, n, dv = DN_CHUNK, DELTA_BLK, DN_DV
    units = [(pp, d, b) for pp in range(DELTA_PAIRS) for d in range(2) for b in range(DELTA_GROUP // 2)]
    st = {u: {} for u in units}
    neg_inf = jnp.float32(-jnp.inf)

    def stage_kq():
        for pp, d, b in units:
            s = st[(pp, d, b)]
            rs = pl.ds(pl.multiple_of((bases[d] + b) * n, n), n)
            kd = k_ref[0, rs, pp * DN_DK:(pp + 1) * DN_DK]
            qd = q_ref[0, rs, pp * DN_DK:(pp + 1) * DN_DK]
            kq = _dot_nt(jnp.concatenate([kd, qd], axis=0), kd)
            kk = kq[:n]
            qk = kq[n:]
            kf = kd.astype(F32)
            qf = qd.astype(F32)
            bg = bg_ref[bases[d] + b]
            a_list = []
            for h in range(2):
                row = 8 * pp + 2 * d + h
                brow = bg[row:row + 1, :]
                grow = bg[row + 4:row + 5, :]
                gc_col = jnp.sum(jnp.where(mk["cum"][d], grow, 0.0), axis=1, keepdims=True)
                gc_row = jnp.sum(jnp.where(mk["eye"], gc_col, 0.0), axis=0, keepdims=True)
                bcol = jnp.sum(jnp.where(mk["eye"], brow, 0.0), axis=1, keepdims=True)
                dec = jnp.exp(jnp.where(mk["incl"][d], gc_col - gc_row, neg_inf))
                a = jnp.where(mk["strict"][d], kk * bcol * dec, 0.0)
                a_list.append(a[:c] + a[c:])
                gl = (gc_row[:, c - 1:c], gc_row[:, n - 1:n]) if d == 0 else (gc_row[:, 0:1], gc_row[:, c:c + 1])
                gl_col = jnp.where(mk["top"], gl[0], gl[1])
                e1 = jnp.exp(gc_col)
                e2 = jnp.exp(gl_col - gc_col)
                vf = v_ref[0, rs, (2 * pp + h) * dv:(2 * pp + h + 1) * dv].astype(F32)
                s[h] = dict(rhs=jnp.concatenate([vf * bcol, kf * (bcol * e1)], axis=1).astype(BF16),
                            qe=(qf * e1).astype(BF16), e2=jnp.broadcast_to(e2, (n, dv)),
                            qkd=(qk * dec).astype(BF16), eg=[jnp.exp(gl[0]), jnp.exp(gl[1])])
            s["x"] = -jnp.concatenate(a_list, axis=1)
            s["q"] = mk["eye4"] + s["x"]

    def stage_t_first():
        for u in units:
            s = st[u]
            xb = s["x"].astype(BF16)
            s["x"] = _bdot(xb, _block_diag4(xb, mk))

    def stage_t_mid():
        for u in units:
            s = st[u]
            xb = s["x"].astype(BF16)
            y = _bdot(jnp.concatenate([s["q"].astype(BF16), xb], axis=0), _block_diag4(xb, mk))
            s["q"] = s["q"] + y[:c]
            s["x"] = y[c:]

    def stage_t_last():
        for u in units:
            s = st[u]
            s["q"] = s["q"] + _bdot(s["q"].astype(BF16), _block_diag4(s["x"].astype(BF16), mk))

    def stage_uw():
        for u in units:
            s = st[u]
            for h in range(2):
                th = s["q"][:, h * n:(h + 1) * n].astype(BF16)
                zero = jnp.zeros((), BF16)
                t_bd = jnp.concatenate([jnp.where(mk["left"], th, zero), jnp.where(mk["left"], zero, th)], axis=0)
                uw = _bdot(t_bd, s[h]["rhs"])
                s[h]["u"] = uw[:, :dv]
                s[h]["w"] = uw[:, dv:].astype(BF16)

    stages = [stage_kq, stage_t_first] + [stage_t_mid] * (CHUNK_SHIFT - 2) + [stage_t_last, stage_uw]
    return stages, st


def _delta_store_local(st, slot, u_s, w_s, qe_s, e2_s, qkd_s, eg_s):
    for pp in range(DELTA_PAIRS):
        for d in range(2):
            ch = slot * (2 * DELTA_PAIRS) + 2 * pp + d
            blocks = [st[(pp, d, b)] for b in range(DELTA_GROUP // 2)]
            for name, ref in (("u", u_s), ("w", w_s), ("qe", qe_s), ("e2", e2_s), ("qkd", qkd_s)):
                ref[ch] = jnp.concatenate(
                    [jnp.concatenate([blk[0][name], blk[1][name]], axis=1) for blk in blocks], axis=0)
            for b, blk in enumerate(blocks):
                for half in range(2):
                    j = 2 * b + half
                    eg_s[ch, j:j + 1, :] = jnp.concatenate(
                        [jnp.broadcast_to(blk[h]["eg"][half], (1, DN_DV)) for h in range(2)], axis=1)


def _delta_scan(k_ref, slot, u_s, w_s, qe_s, e2_s, qkd_s, eg_s, s_ref, o_ref, acc_ref, bases):
    c, dv = DN_CHUNK, DN_DV
    n_chains = 2 * DELTA_PAIRS
    first = slot * n_chains
    stages = []
    for jf in range(DELTA_GROUP):
        wq = {}

        def chunk_of(ch, jf=jf):
            return jf if ch % 2 == 0 else DELTA_GROUP - 1 - jf

        def stage1(wq=wq, chunk_of=chunk_of):
            for ch in range(n_chains):
                j = chunk_of(ch)
                rl = slice(j * c, (j + 1) * c)
                lhs = jnp.concatenate([w_s[first + ch, rl, :], qe_s[first + ch, rl, :]], axis=0)
                wq[ch] = _bdot(lhs, _block_diag2(s_ref[ch].astype(BF16)))

        def stage2(wq=wq, chunk_of=chunk_of):
            zpad = jnp.zeros((c, dv), BF16)
            for ch in range(n_chains):
                pp, d = ch // 2, ch % 2
                j = chunk_of(ch)
                rl = slice(j * c, (j + 1) * c)
                r0 = pl.multiple_of(bases[d] * DELTA_BLK + j * c, c)
                v_new = u_s[first + ch, rl, :] - wq[ch][:c]
                vb = v_new.astype(BF16)
                pads = []
                for h in range(2):
                    vh = vb[:, h * dv:(h + 1) * dv]
                    pads.append(jnp.concatenate([vh, zpad] if j % 2 == 0 else [zpad, vh], axis=0))
                v_bd = _block_diag2(jnp.concatenate(pads, axis=1))
                o = wq[ch][c:] + _bdot(qkd_s[first + ch, rl, :], v_bd)
                kc = k_ref[0, pl.ds(r0, c), pp * DN_DK:(pp + 1) * DN_DK]
                decay = eg_s[first + ch, j:j + 1, :]
                s_ref[ch] = s_ref[ch] * decay + _dot_tn(kc, (v_new * e2_s[first + ch, rl, :]).astype(BF16))
                if d == 0:
                    o_ref[0, pl.ds(r0, c), pp * PAIR_W:(pp + 1) * PAIR_W] = o.astype(o_ref.dtype)
                else:
                    acc_ref[pp, pl.ds(r0, c), :] = o.astype(acc_ref.dtype)

        stages += [stage1, stage2]
    return stages


def _delta_kernel(q_ref, k_ref, v_ref, ba_ref, alog_ref, dtb_ref, o_ref,
                  bg_ref, s_ref, acc_ref, u_s, w_s, qe_s, e2_s, qkd_s, eg_s):
    seq = q_ref.shape[1]
    n_groups = seq // DELTA_ROWS
    n = DELTA_BLK

    raw = ba_ref[0, 0]
    xx = raw + dtb_ref[0]
    softplus = jnp.maximum(xx, 0.0) + jnp.log(1.0 + jnp.exp(-jnp.abs(xx)))
    g = -jnp.exp(alog_ref[0]) * softplus
    is_beta = jnp.bitwise_and(lax.broadcasted_iota(jnp.int32, (raw.shape[0], 1), 0), 7) < 4
    bg = jnp.where(is_beta, jax.nn.sigmoid(raw), g)
    for i in range(seq // n):
        bg_ref[i] = bg[:, i * n:(i + 1) * n]
    s_ref[...] = jnp.zeros(s_ref.shape, F32)

    mk = _delta_masks()
    local_scratch = (u_s, w_s, qe_s, e2_s, qkd_s, eg_s)

    def group_bases(t):
        return (t * (DELTA_GROUP // 2), (n_groups - 1 - t) * (DELTA_GROUP // 2))

    stages, st = _delta_local(q_ref, k_ref, v_ref, bg_ref, group_bases(0), mk)
    for stage in stages:
        stage()
    _delta_store_local(st, 0, *local_scratch)

    def trip(t, carry):
        nxt = jnp.minimum(t + 1, n_groups - 1)
        slot = jnp.bitwise_and(t, 1)
        l_stages, l_st = _delta_local(q_ref, k_ref, v_ref, bg_ref, group_bases(nxt), mk)
        s_stages = _delta_scan(k_ref, slot, *local_scratch, s_ref, o_ref, acc_ref, group_bases(t))
        for a, b in zip(l_stages, s_stages):
            b()
            a()
        _delta_store_local(l_st, 1 - slot, *local_scratch)
        return carry

    lax.fori_loop(0, n_groups, trip, 0)

    rows = min(512, seq)
    for r in range(0, seq, rows):
        for pp in range(DELTA_PAIRS):
            cols = slice(pp * PAIR_W, (pp + 1) * PAIR_W)
            o = o_ref[0, r:r + rows, cols].astype(F32) + acc_ref[pp, r:r + rows, :].astype(F32)
            o_ref[0, r:r + rows, cols] = o.astype(o_ref.dtype)


def _dn_delta(qkv_n, ba_r, alog_r, dtb_r):
    b, seq, _ = qkv_n.shape
    assert seq % DELTA_ROWS == 0 and len(_delta_local(None, None, None, None, None, None)[0]) == 2 * DELTA_GROUP
    n_steps = DN_NK // DELTA_PAIRS
    qk_w = DELTA_PAIRS * DN_DK
    v_w = DELTA_PAIRS * PAIR_W
    k_tile0 = DN_K_DIM // qk_w
    v_tile0 = 2 * DN_K_DIM // v_w
    n_rows = 8 * DELTA_PAIRS
    n_chains = 2 * DELTA_PAIRS
    grp = lambda dt: pltpu.VMEM((2 * n_chains, DELTA_ROWS, PAIR_W), dt)
    block_bytes = 2 * (2 * seq * qk_w * 2 + 2 * seq * v_w * 2 + n_rows * seq * 4)
    scratch_bytes = (DELTA_PAIRS * seq * PAIR_W * 2 + 2 * n_chains * DELTA_ROWS * PAIR_W * (4 + 2 + 2 + 4 + 2)
                     + n_chains * DN_DK * PAIR_W * 4 + n_rows * seq * 4 + 2 * n_chains * 8 * PAIR_W * 4)
    vmem = block_bytes + scratch_bytes + DELTA_SPILL_BYTES
    return pl.pallas_call(
        _delta_kernel,
        grid=(b, n_steps),
        in_specs=[pl.BlockSpec((1, seq, qk_w), lambda i, p: (i, 0, p)),
                  pl.BlockSpec((1, seq, qk_w), lambda i, p: (i, 0, k_tile0 + p)),
                  pl.BlockSpec((1, seq, v_w), lambda i, p: (i, 0, v_tile0 + p)),
                  pl.BlockSpec((1, 1, n_rows, seq), lambda i, p: (i, p, 0, 0)),
                  pl.BlockSpec((1, n_rows, 1), lambda i, p: (p, 0, 0)),
                  pl.BlockSpec((1, n_rows, 1), lambda i, p: (p, 0, 0))],
        out_specs=pl.BlockSpec((1, seq, v_w), lambda i, p: (i, 0, p)),
        out_shape=jax.ShapeDtypeStruct((b, seq, DN_V_DIM), BF16),
        scratch_shapes=[pltpu.VMEM((seq // DELTA_BLK, n_rows, DELTA_BLK), F32),
                        pltpu.VMEM((n_chains, DN_DK, PAIR_W), F32),
                        pltpu.VMEM((DELTA_PAIRS, seq, PAIR_W), BF16),
                        grp(F32), grp(BF16), grp(BF16), grp(F32), grp(BF16),
                        pltpu.VMEM((2 * n_chains, 8, PAIR_W), F32)],
        compiler_params=_cparams("parallel", "parallel", vmem=vmem),
        name="dn_delta",
    )(qkv_n, qkv_n, qkv_n, ba_r, alog_r, dtb_r)


def _step_rows(t):
    n_steps = DN_NK // DELTA_PAIRS
    t = t.astype(F32).reshape(2, n_steps, DELTA_PAIRS, 2).transpose(1, 2, 0, 3).reshape(n_steps, DELTA_PAIRS, 4)
    return jnp.concatenate([jnp.zeros_like(t), t], axis=2).reshape(n_steps, 8 * DELTA_PAIRS, 1)


def _gated_proj_ln_kernel(o_ref, z_ref, nw_ref, w_ref, x_ref, g_ref, b_ref, out_ref):
    nw = nw_ref[...]
    cols = []
    for h in range(DN_NV):
        sl = slice(h * DN_DV, (h + 1) * DN_DV)
        o = o_ref[:, sl].astype(F32)
        ms = jnp.mean(o * o, axis=-1, keepdims=True)
        cols.append((o * lax.rsqrt(ms + RMS_EPS) * nw * _silu(z_ref[:, sl].astype(F32))).astype(BF16))
    y = _bdot(jnp.concatenate(cols, axis=1), w_ref[...])
    out_ref[...] = _layernorm(DEEPNORM_ALPHA * x_ref[...] + y, g_ref[...], b_ref[...])


def _gated_proj_ln(o, qkvz, norm_w, w, x, g, b, tm=512):
    t, k = o.shape
    d = w.shape[1]
    tm = min(tm, t)
    z_tile = DN_CONV_CH // k
    return pl.pallas_call(
        _gated_proj_ln_kernel,
        grid=(t // tm,),
        in_specs=[pl.BlockSpec((tm, k), lambda i: (i, 0)),
                  pl.BlockSpec((tm, k), lambda i: (i, z_tile)),
                  pl.BlockSpec((1, DN_DV), lambda i: (0, 0)),
                  pl.BlockSpec((k, d), lambda i: (0, 0)),
                  pl.BlockSpec((tm, d), lambda i: (i, 0)),
                  pl.BlockSpec((1, d), lambda i: (0, 0)),
                  pl.BlockSpec((1, d), lambda i: (0, 0))],
        out_specs=pl.BlockSpec((tm, d), lambda i: (i, 0)),
        out_shape=jax.ShapeDtypeStruct((t, d), F32),
        compiler_params=_cparams("parallel"),
        name="gated_proj_ln",
    )(o, qkvz, norm_w.reshape(1, DN_DV), w, x, g.reshape(1, d), b.reshape(1, d))


def _deltanet_layer(xf, bsz, seq, w_in, conv_w, a_log, dt_bias, norm_w, w_out, ln_g, ln_b):
    t = xf.shape[0]
    n_qkvz = DN_CONV_CH + DN_V_DIM
    n_steps = DN_NK // DELTA_PAIRS
    w_bf = w_in.astype(BF16)
    qkvz = _matmul(xf, w_bf[:, :n_qkvz], BF16, tm=1024, tn=1536)
    ba = _matmul(xf, w_bf[:, n_qkvz:], F32, tm=1024, tn=4 * DN_NV)
    ba_r = ba.reshape(bsz, seq, 2, 2, n_steps, DELTA_PAIRS, 2).transpose(0, 4, 5, 2, 3, 6, 1)
    ba_r = ba_r.reshape(bsz, n_steps, 8 * DELTA_PAIRS, seq)
    qkv_n = _dn_conv(qkvz.reshape(bsz, seq, n_qkvz), conv_w)
    o = _dn_delta(qkv_n, ba_r, _step_rows(a_log), _step_rows(dt_bias))
    return _gated_proj_ln(o.reshape(t, DN_V_DIM), qkvz, norm_w, w_out.astype(BF16), xf, ln_g, ln_b)


MLA_QK = MLA_NOPE + MLA_ROPE
MLA_IN_EXT = MLA_QR + MLA_KVR + 2 * LANES
MLA_Q_EXT = MLA_NOPE + 2 * MLA_ROPE
MLA_V_EXT = MLA_V + 16
LOG2E = 1.4426950408889634


def _rope_kernel(pos_ref, invf_ref, sign_ref, cs_ref, sn_ref):
    ang = invf_ref[...] * pos_ref[0]
    cs_ref[0] = jnp.cos(ang)
    sn_ref[0] = jnp.sin(ang) * sign_ref[...]


def _rope_tables(positions):
    b, seq = positions.shape
    half = MLA_ROPE // 2
    inv_freq = 1.0 / (MLA_ROPE_THETA ** (jnp.arange(0, MLA_ROPE, 2, dtype=F32) / MLA_ROPE))
    invf = jnp.concatenate([inv_freq, inv_freq]).reshape(MLA_ROPE, 1)
    sign = jnp.concatenate([-jnp.ones((half,), F32), jnp.ones((half,), F32)]).reshape(MLA_ROPE, 1)
    out = jax.ShapeDtypeStruct((b, MLA_ROPE, seq), F32)
    return pl.pallas_call(
        _rope_kernel,
        grid=(b,),
        in_specs=[pl.BlockSpec((1, 1, seq), lambda i: (i, 0, 0)),
                  pl.BlockSpec((MLA_ROPE, 1), lambda i: (0, 0)),
                  pl.BlockSpec((MLA_ROPE, 1), lambda i: (0, 0))],
        out_specs=[pl.BlockSpec((1, MLA_ROPE, seq), lambda i: (i, 0, 0)),
                   pl.BlockSpec((1, MLA_ROPE, seq), lambda i: (i, 0, 0))],
        out_shape=[out, out],
        compiler_params=_cparams("parallel"),
        name="rope_tables",
    )(positions.astype(F32).reshape(b, 1, seq), invf, sign)


def _rms(x, w):
    return x * lax.rsqrt(jnp.mean(x * x, axis=-1, keepdims=True) + RMS_EPS) * w


def _mla_in_kernel(x_ref, w_ref, qn_ref, kvn_ref, cs_ref, sn_ref, cq_ref, ckv_ref, kr_ref):
    h = _bdot(x_ref[...].astype(BF16), w_ref[...])
    cq_ref[...] = _rms(h[:, :MLA_QR], qn_ref[...]).astype(cq_ref.dtype)
    ckv_ref[...] = _rms(h[:, MLA_QR:MLA_QR + MLA_KVR], kvn_ref[...]).astype(ckv_ref.dtype)
    r0 = MLA_QR + MLA_KVR
    kr = h[:, r0:r0 + MLA_ROPE] * cs_ref[...] + h[:, r0 + LANES:r0 + LANES + MLA_ROPE] * sn_ref[...]
    kr_ref[...] = kr.astype(kr_ref.dtype)


def _mla_in(xf, w_ext, q_norm, kv_norm, cs, sn, tm=512):
    t, d = xf.shape
    tm = min(tm, t)
    row = lambda i: (i, 0)
    fix = lambda i: (0, 0)
    return pl.pallas_call(
        _mla_in_kernel,
        grid=(t // tm,),
        in_specs=[pl.BlockSpec((tm, d), row),
                  pl.BlockSpec((d, MLA_IN_EXT), fix),
                  pl.BlockSpec((1, MLA_QR), fix),
                  pl.BlockSpec((1, MLA_KVR), fix),
                  pl.BlockSpec((tm, MLA_ROPE), row),
                  pl.BlockSpec((tm, MLA_ROPE), row)],
        out_specs=[pl.BlockSpec((tm, MLA_QR), row),
                   pl.BlockSpec((tm, MLA_KVR), row),
                   pl.BlockSpec((tm, MLA_ROPE), row)],
        out_shape=[jax.ShapeDtypeStruct((t, MLA_QR), BF16),
                   jax.ShapeDtypeStruct((t, MLA_KVR), BF16),
                   jax.ShapeDtypeStruct((t, MLA_ROPE), BF16)],
        compiler_params=_cparams("parallel"),
        name="mla_in",
    )(xf, w_ext, q_norm.reshape(1, MLA_QR), kv_norm.reshape(1, MLA_KVR), cs, sn)


def _mla_q_kernel(cq_ref, w_ref, cs_ref, sn_ref, q_ref):
    acc = _dot_nt(w_ref[...], cq_ref[...])
    cs = cs_ref[0]
    sn = sn_ref[0]
    scale = MLA_QK ** -0.5 * LOG2E
    for h in range(MLA_H):
        a0 = h * MLA_Q_EXT
        o0 = h * MLA_QK
        q_ref[0, o0:o0 + MLA_NOPE, :] = (acc[a0:a0 + MLA_NOPE] * scale).astype(q_ref.dtype)
        x = acc[a0 + MLA_NOPE:a0 + MLA_NOPE + MLA_ROPE]
        xs = acc[a0 + MLA_NOPE + MLA_ROPE:a0 + MLA_Q_EXT]
        q_ref[0, o0 + MLA_NOPE:o0 + MLA_QK, :] = ((x * cs + xs * sn) * scale).astype(q_ref.dtype)


def _mla_q(cq, wq_t, cs_t, sn_t, bsz, seq, tl=512):
    tl = min(tl, seq)
    nl = seq // tl
    return pl.pallas_call(
        _mla_q_kernel,
        grid=(bsz, nl),
        in_specs=[pl.BlockSpec((tl, MLA_QR), lambda b, i: (b * nl + i, 0)),
                  pl.BlockSpec((MLA_H * MLA_Q_EXT, MLA_QR), lambda b, i: (0, 0)),
                  pl.BlockSpec((1, MLA_ROPE, tl), lambda b, i: (b, 0, i)),
                  pl.BlockSpec((1, MLA_ROPE, tl), lambda b, i: (b, 0, i))],
        out_specs=pl.BlockSpec((1, MLA_H * MLA_QK, tl), lambda b, i: (b, 0, i)),
        out_shape=jax.ShapeDtypeStruct((bsz, MLA_H * MLA_QK, seq), BF16),
        compiler_params=_cparams("parallel", "parallel"),
        name="mla_q",
    )(cq, wq_t, cs_t, sn_t)


def _mla_kv_kernel(ckv_ref, kr_ref, wk_ref, wv_ref, k_ref, v_ref):
    ckv = ckv_ref[...]
    kn = _bdot(ckv, wk_ref[...])
    kr = kr_ref[...]
    for h in range(MLA_H):
        k_ref[0, h] = jnp.concatenate(
            [kn[:, h * MLA_NOPE:(h + 1) * MLA_NOPE].astype(k_ref.dtype), kr], axis=1)
    v_t = _dot_nt(wv_ref[...], ckv).astype(v_ref.dtype)
    ones = jnp.ones((MLA_V_EXT - MLA_V, v_t.shape[1]), v_ref.dtype)
    for h in range(MLA_H):
        v_ref[0, h * MLA_V_EXT:h * MLA_V_EXT + MLA_V, :] = v_t[h * MLA_V:(h + 1) * MLA_V]
        v_ref[0, h * MLA_V_EXT + MLA_V:(h + 1) * MLA_V_EXT, :] = ones


def _mla_kv(ckv, kr, wk, wv_t, bsz, seq, tl=512):
    tl = min(tl, seq)
    nl = seq // tl
    return pl.pallas_call(
        _mla_kv_kernel,
        grid=(bsz, nl),
        in_specs=[pl.BlockSpec((tl, MLA_KVR), lambda b, i: (b * nl + i, 0)),
                  pl.BlockSpec((tl, MLA_ROPE), lambda b, i: (b * nl + i, 0)),
                  pl.BlockSpec((MLA_KVR, MLA_H * MLA_NOPE), lambda b, i: (0, 0)),
                  pl.BlockSpec((MLA_H * MLA_V, MLA_KVR), lambda b, i: (0, 0))],
        out_specs=[pl.BlockSpec((1, MLA_H, tl, MLA_QK), lambda b, i: (b, 0, i, 0)),
                   pl.BlockSpec((1, MLA_H * MLA_V_EXT, tl), lambda b, i: (b, 0, i))],
        out_shape=[jax.ShapeDtypeStruct((bsz, MLA_H, seq, MLA_QK), BF16),
                   jax.ShapeDtypeStruct((bsz, MLA_H * MLA_V_EXT, seq), BF16)],
        compiler_params=_cparams("parallel", "parallel"),
        name="mla_kv",
    )(ckv, kr, wk, wv_t)


ATTN_TQ = 2048
ATTN_TK = 512


def _attn_kernel(q_ref, k_ref, v_ref, o_ref):
    q_t = q_ref[0]
    tq = q_t.shape[1]
    seq = k_ref.shape[2]
    tk = min(ATTN_TK, seq)
    m = jnp.full((1, tq), -jnp.inf, F32)
    acc = jnp.zeros((MLA_V_EXT, tq), F32)
    n_k = seq // tk
    scores = lambda j: _bdot(k_ref[0, 0, j * tk:(j + 1) * tk, :], q_t)
    s_next = scores(0)
    for j in range(n_k):
        s = s_next
        if j + 1 < n_k:
            s_next = scores(j + 1)
        m_new = jnp.maximum(m, jnp.max(s, axis=0, keepdims=True))
        p = jnp.exp2(s - m_new).astype(BF16)
        acc = jnp.exp2(m - m_new) * acc + _bdot(v_ref[0, :, j * tk:(j + 1) * tk], p)
        m = m_new
    o_ref[0] = (acc[:MLA_V] / acc[MLA_V:MLA_V + 1]).T.astype(o_ref.dtype)


def _mla_attn(q_t, k, v_t, bsz, seq):
    tq = min(ATTN_TQ, seq)
    return pl.pallas_call(
        _attn_kernel,
        grid=(bsz, MLA_H, seq // tq),
        in_specs=[pl.BlockSpec((1, MLA_QK, tq), lambda b, h, i: (b, h, i)),
                  pl.BlockSpec((1, 1, seq, MLA_QK), lambda b, h, i: (b, h, 0, 0)),
                  pl.BlockSpec((1, MLA_V_EXT, seq), lambda b, h, i: (b, h, 0))],
        out_specs=pl.BlockSpec((1, tq, MLA_V), lambda b, h, i: (b, i, h)),
        out_shape=jax.ShapeDtypeStruct((bsz, seq, MLA_H * MLA_V), BF16),
        compiler_params=_cparams("parallel", "parallel", "parallel"),
        name="mla_attn",
    )(q_t, k, v_t)


def _swap_halves(t):
    half = t.shape[-1] // 2
    return jnp.concatenate([t[..., half:], t[..., :half]], axis=-1)


def _mla_layer(xf, bsz, seq, cs, sn, cs_t, sn_t, w_in, q_norm, w_qb, kv_norm, w_kvb, w_out, ln_g, ln_b):
    t, d = xf.shape
    r0 = MLA_QR + MLA_KVR
    w_rope = w_in[:, r0:]
    pad = jnp.zeros((d, LANES - MLA_ROPE), w_in.dtype)
    w_ext = jnp.concatenate([w_in, pad, _swap_halves(w_rope), pad], axis=1).astype(BF16)
    wq = w_qb.reshape(MLA_QR, MLA_H, MLA_QK)
    wq_rope = wq[:, :, MLA_NOPE:]
    wq_t = jnp.concatenate([wq, _swap_halves(wq_rope)], axis=-1).reshape(MLA_QR, MLA_H * MLA_Q_EXT).T.astype(BF16)
    wkv = w_kvb.reshape(MLA_KVR, MLA_H, MLA_NOPE + MLA_V)
    wk = wkv[:, :, :MLA_NOPE].reshape(MLA_KVR, MLA_H * MLA_NOPE).astype(BF16)
    wv_t = wkv[:, :, MLA_NOPE:].reshape(MLA_KVR, MLA_H * MLA_V).T.astype(BF16)

    cq, ckv, kr = _mla_in(xf, w_ext, q_norm, kv_norm, cs, sn)
    q_t = _mla_q(cq, wq_t, cs_t, sn_t, bsz, seq)
    k, v_t = _mla_kv(ckv, kr, wk, wv_t, bsz, seq)
    o = _mla_attn(q_t, k, v_t, bsz, seq)
    return _proj_ln(o.reshape(t, MLA_H * MLA_V), w_out.astype(BF16), xf, ln_g, ln_b)


def _ind(mask):
    return jnp.where(mask, 1.0, 0.0)


def _route_rows(logits_t, bias_col):
    scores = jax.nn.sigmoid(logits_t)
    sel = scores + bias_col
    s = [sel[e:e + 1] for e in range(N_EXPERTS)]
    sc = [scores[e:e + 1] for e in range(N_EXPERTS)]

    def ranks(vals):
        rank = [jnp.zeros_like(vals[0]) for _ in vals]
        for i in range(len(vals)):
            for k in range(i + 1, len(vals)):
                k_wins = _ind(vals[k] > vals[i])
                rank[i] = rank[i] + k_wins
                rank[k] = rank[k] + (1.0 - k_wins)
        return rank

    top2, gsum = [], []
    for g in range(N_GROUPS):
        idx = range(g * EXPERTS_PER_GROUP, (g + 1) * EXPERTS_PER_GROUP)
        rk = ranks([s[e] for e in idx])
        t2 = [r < 2.0 for r in rk]
        top2.append(t2)
        gsum.append(sum(jnp.where(t, s[e], 0.0) for t, e in zip(t2, idx)))
    won = [r < 0.5 for r in ranks(gsum)]
    w = [sum(jnp.where(won[g], jnp.where(top2[g][e], sc[g * EXPERTS_PER_GROUP + e], 0.0), 0.0)
             for g in range(N_GROUPS)) for e in range(EXPERTS_PER_GROUP)]
    tot = sum(w)
    return [we / tot for we in w], [_ind(m) for m in won]


MOE_BLOCK = 128
GATE_ROWS = 8


def _router_weights(router_w):
    hi, lo = _split_bf16(router_w.astype(F32).T)
    return jnp.concatenate([hi, lo], axis=0)


def _split_bf16(v):
    hi = v.astype(BF16)
    return hi, (v - hi.astype(F32)).astype(BF16)


def _rows8(rows):
    pad = [jnp.zeros_like(rows[0])] * (GATE_ROWS - len(rows))
    return jnp.concatenate(list(rows) + pad, axis=0)


def _moe_kernel(x_ref, rw_ref, rb_ref, wg_ref, wu_ref, wd_ref, g_ref, b_ref, o_ref,
                xs_ref, pt_ref, gate_ref, acc_ref, seg_ref):
    j = pl.program_id(1)
    tm = x_ref.shape[0]

    @pl.when(j == 0)
    def _():
        x = x_ref[...]
        xh, xl = _split_bf16(x)
        rw = rw_ref[...]
        lg = _dot_nt(rw, xh)
        logits_t = lg[:N_EXPERTS] + lg[N_EXPERTS:] + _dot_nt(rw[:N_EXPERTS], xl)
        gates, member = _route_rows(logits_t, rb_ref[...])
        ri = lax.broadcasted_iota(jnp.int32, (tm, tm), 0)
        ci = lax.broadcasted_iota(jnp.int32, (tm, tm), 1)
        utri = jnp.where(ri <= ci, 1.0, 0.0).astype(BF16)
        prefix = _bdot(_rows8(member).astype(BF16), utri)
        cnt = [prefix[g:g + 1, tm - 1:tm] for g in range(N_GROUPS)]
        off = [jnp.zeros((1, 1), F32)]
        for g in range(1, N_GROUPS):
            off.append(off[-1] + cnt[g - 1])
        pos_row = sum(member[g] * (off[g] + prefix[g:g + 1] - 1.0) for g in range(N_GROUPS))
        pos_col = jnp.sum(jnp.where(ri == ci, pos_row, 0.0), axis=1, keepdims=True)
        p = jnp.where(pos_row == ri.astype(F32), 1.0, 0.0).astype(BF16)
        pt_ref[...] = jnp.where(pos_col == ci.astype(F32), 1.0, 0.0).astype(BF16)
        xs_ref[...] = _bdot(p, xh).astype(BF16)
        gh, gl = _split_bf16(_rows8(gates))
        gate_ref[...] = _dot_nt(p, gh) + _dot_nt(p, gl)
        for g in range(N_GROUPS):
            seg_ref[g] = off[g][0, 0].astype(jnp.int32)
            seg_ref[N_GROUPS + g] = cnt[g][0, 0].astype(jnp.int32)
        acc_ref[...] = jnp.zeros(acc_ref.shape, F32)

    seg_lo = seg_ref[j]
    seg_hi = seg_lo + seg_ref[N_GROUPS + j]
    for r in range(0, tm, MOE_BLOCK):
        @pl.when(jnp.logical_and(seg_lo < r + MOE_BLOCK, seg_hi > r))
        def _(r=r):
            xb = xs_ref[r:r + MOE_BLOCK, :]
            row = r + lax.broadcasted_iota(jnp.int32, (MOE_BLOCK, 1), 0)
            mine = jnp.logical_and(row >= seg_lo, row < seg_hi)
            gates = jnp.where(mine, gate_ref[r:r + MOE_BLOCK, :], 0.0)
            hid = []
            for e in range(EXPERTS_PER_GROUP):
                hg = _bdot(xb, wg_ref[e])
                hu = _bdot(xb, wu_ref[e])
                hid.append((_silu(hg) * hu * gates[:, e:e + 1]).astype(BF16))
            acc_ref[r:r + MOE_BLOCK, :] += _bdot(jnp.concatenate(hid, axis=1), wd_ref[...])

    @pl.when(j == N_GROUPS - 1)
    def _():
        y = _bdot(pt_ref[...], acc_ref[...].astype(BF16))
        o_ref[...] = _layernorm(DEEPNORM_ALPHA * x_ref[...] + y, g_ref[...], b_ref[...])


def _moe_layer(xf, rw_t, rb_t, w_gate, w_up, w_down, ln_g, ln_b, tm=512):
    t, d = xf.shape
    tm = min(tm, t)
    f = D_FF_EXPERT
    gf = EXPERTS_PER_GROUP * f
    row = lambda i, j: (i, 0)
    fix = lambda i, j: (0, 0)
    return pl.pallas_call(
        _moe_kernel,
        grid=(t // tm, N_GROUPS),
        in_specs=[pl.BlockSpec((tm, d), row),
                  pl.BlockSpec((2 * N_EXPERTS, d), fix),
                  pl.BlockSpec((N_EXPERTS, 1), fix),
                  pl.BlockSpec((EXPERTS_PER_GROUP, d, f), lambda i, j: (j, 0, 0)),
                  pl.BlockSpec((EXPERTS_PER_GROUP, d, f), lambda i, j: (j, 0, 0)),
                  pl.BlockSpec((gf, d), lambda i, j: (j, 0)),
                  pl.BlockSpec((1, d), fix),
                  pl.BlockSpec((1, d), fix)],
        out_specs=pl.BlockSpec((tm, d), row),
        out_shape=jax.ShapeDtypeStruct((t, d), F32),
        scratch_shapes=[pltpu.VMEM((tm, d), BF16),
                        pltpu.VMEM((tm, tm), BF16),
                        pltpu.VMEM((tm, GATE_ROWS), F32),
                        pltpu.VMEM((tm, d), F32),
                        pltpu.SMEM((2 * N_GROUPS,), jnp.int32)],
        compiler_params=_cparams("parallel", "arbitrary"),
        name="moe",
    )(xf, rw_t, rb_t, w_gate.astype(BF16), w_up.astype(BF16),
      w_down.astype(BF16).reshape(N_EXPERTS * f, d), ln_g.reshape(1, d), ln_b.reshape(1, d))


def kernel(x, positions, dn_w_in, dn_conv_w, dn_a_log, dn_dt_bias, dn_norm_w, dn_w_out, mla_w_in, mla_q_norm, mla_w_qb, mla_kv_norm, mla_w_kvb, mla_w_out, ln1_g, ln1_b, ln2_g, ln2_b, router_w, router_bias, moe_w_gate, moe_w_up, moe_w_down):
    bsz, seq, d = x.shape
    t = bsz * seq
    xf = x.reshape(t, d)
    cs_t, sn_t = _rope_tables(positions)
    cs = cs_t.transpose(0, 2, 1).reshape(t, MLA_ROPE)
    sn = sn_t.transpose(0, 2, 1).reshape(t, MLA_ROPE)
    rw_t = _router_weights(router_w)
    rb_t = router_bias.astype(F32).reshape(N_EXPERTS, 1)
    for i in range(DEPTH):
        j = i // 2
        if i % 2 == 0:
            xf = _deltanet_layer(xf, bsz, seq, dn_w_in[j], dn_conv_w[j], dn_a_log[j], dn_dt_bias[j],
                                 dn_norm_w[j], dn_w_out[j], ln1_g[i], ln1_b[i])
        else:
            xf = _mla_layer(xf, bsz, seq, cs, sn, cs_t, sn_t, mla_w_in[j], mla_q_norm[j], mla_w_qb[j],
                            mla_kv_norm[j], mla_w_kvb[j], mla_w_out[j], ln1_g[i], ln1_b[i])
        xf = _moe_layer(xf, rw_t, rb_t, moe_w_gate[i], moe_w_up[i], moe_w_down[i], ln2_g[i], ln2_b[i])
    return xf.reshape(bsz, seq, d)
```

```python
import functools

import jax
import jax.numpy as jnp
from jax import lax
from jax.experimental import pallas as pl
from jax.experimental.pallas import tpu as pltpu

F32 = jnp.float32
BF16 = jnp.bfloat16

DEPTH = 4
DN_NK = 8
DN_NV = 16
DN_DK = 128
DN_DV = 128
DN_K_DIM = DN_NK * DN_DK
DN_V_DIM = DN_NV * DN_DV
DN_CONV_CH = 2 * DN_K_DIM + DN_V_DIM
DN_CONV_W = 5
DN_CHUNK = 64
MLA_H = 8
MLA_NOPE = 128
MLA_ROPE = 64
MLA_V = 128
MLA_QR = 384
MLA_KVR = 256
MLA_ROPE_THETA = 10000.0
N_EXPERTS = 16
N_GROUPS = 4
EXPERTS_PER_GROUP = N_EXPERTS // N_GROUPS
D_FF_EXPERT = 256
DEEPNORM_ALPHA = (2 * DEPTH) ** 0.25
LN_EPS = 1e-5
RMS_EPS = 1e-6

LANES = 128
VMEM_LIMIT = 48 * 1024 * 1024


def _cparams(*sems, vmem=VMEM_LIMIT):
    return pltpu.CompilerParams(dimension_semantics=sems, vmem_limit_bytes=vmem)


def _bdot(a, b):
    return jnp.dot(a, b, preferred_element_type=F32)


def _dot_nt(a, b):
    return lax.dot_general(a, b, (((1,), (1,)), ((), ())), preferred_element_type=F32)


def _dot_tn(a, b):
    return lax.dot_general(a, b, (((0,), (0,)), ((), ())), preferred_element_type=F32)


def _silu(x):
    return x * jax.nn.sigmoid(x)


def _mm_kernel(x_ref, w_ref, o_ref):
    o_ref[...] = _bdot(x_ref[...].astype(BF16), w_ref[...]).astype(o_ref.dtype)


def _matmul(x, w, out_dtype, tm, tn):
    t, k = x.shape
    n = w.shape[1]
    tm = min(tm, t)
    return pl.pallas_call(
        _mm_kernel,
        grid=(n // tn, t // tm),
        in_specs=[pl.BlockSpec((tm, k), lambda j, i: (i, 0)),
                  pl.BlockSpec((k, tn), lambda j, i: (0, j))],
        out_specs=pl.BlockSpec((tm, tn), lambda j, i: (i, j)),
        out_shape=jax.ShapeDtypeStruct((t, n), out_dtype),
        compiler_params=_cparams("parallel", "parallel"),
        name="proj",
    )(x, w)


def _layernorm(h, g, b):
    mu = jnp.mean(h, axis=-1, keepdims=True)
    hc = h - mu
    var = jnp.mean(hc * hc, axis=-1, keepdims=True)
    return hc * lax.rsqrt(var + LN_EPS) * g + b


def _proj_ln_kernel(a_ref, w_ref, x_ref, g_ref, b_ref, o_ref):
    y = _bdot(a_ref[...], w_ref[...])
    o_ref[...] = _layernorm(DEEPNORM_ALPHA * x_ref[...] + y, g_ref[...], b_ref[...])


def _proj_ln(a, w, x, g, b, tm=512):
    t, k = a.shape
    d = w.shape[1]
    tm = min(tm, t)
    return pl.pallas_call(
        _proj_ln_kernel,
        grid=(t // tm,),
        in_specs=[pl.BlockSpec((tm, k), lambda i: (i, 0)),
                  pl.BlockSpec((k, d), lambda i: (0, 0)),
                  pl.BlockSpec((tm, d), lambda i: (i, 0)),
                  pl.BlockSpec((1, d), lambda i: (0, 0)),
                  pl.BlockSpec((1, d), lambda i: (0, 0))],
        out_specs=pl.BlockSpec((tm, d), lambda i: (i, 0)),
        out_shape=jax.ShapeDtypeStruct((t, d), F32),
        compiler_params=_cparams("parallel"),
        name="proj_ln",
    )(a, w, x, g.reshape(1, d), b.reshape(1, d))


CONV_PAD_ROWS = 8
CONV_ROWS = 512


def _conv_kernel(x_ref, w_ref, o_ref, pad_ref):
    c = pl.program_id(1)
    seq = x_ref.shape[1]
    zeros = jnp.zeros((CONV_PAD_ROWS, LANES), F32)
    pad_ref[0:CONV_PAD_ROWS, :] = zeros
    pad_ref[seq + CONV_PAD_ROWS:seq + 2 * CONV_PAD_ROWS, :] = zeros
    pad_ref[CONV_PAD_ROWS:seq + CONV_PAD_ROWS, :] = x_ref[0].astype(F32)
    w = w_ref[...]
    is_q = c < DN_NK
    is_v = c >= 2 * DN_NK
    rows = min(CONV_ROWS, seq)
    half = DN_CONV_W // 2
    for r in range(0, seq, rows):
        acc = None
        for j in range(DN_CONV_W):
            lo = CONV_PAD_ROWS + r + j - half
            term = w[j:j + 1, :] * pad_ref[lo:lo + rows, :]
            acc = term if acc is None else acc + term
        y = _silu(acc)
        nrm = lax.rsqrt(jnp.sum(y * y, axis=-1, keepdims=True) + RMS_EPS)
        nrm = nrm * jnp.where(is_q, DN_DK ** -0.5, 1.0)
        scale = jnp.where(is_v, jnp.ones_like(nrm), nrm)
        o_ref[0, r:r + rows, :] = (y * scale).astype(o_ref.dtype)


def _dn_conv(qkvz, conv_w):
    b, seq, _ = qkvz.shape
    n_tiles = DN_CONV_CH // LANES
    return pl.pallas_call(
        _conv_kernel,
        grid=(b, n_tiles),
        in_specs=[pl.BlockSpec((1, seq, LANES), lambda i, c: (i, 0, c)),
                  pl.BlockSpec((DN_CONV_W, LANES), lambda i, c: (0, c))],
        out_specs=pl.BlockSpec((1, seq, LANES), lambda i, c: (i, 0, c)),
        out_shape=jax.ShapeDtypeStruct((b, seq, DN_CONV_CH), BF16),
        scratch_shapes=[pltpu.VMEM((seq + 2 * CONV_PAD_ROWS, LANES), F32)],
        compiler_params=_cparams("parallel", "parallel"),
        name="dn_conv",
    )(qkvz, conv_w)


DELTA_GROUP = 4
DELTA_PAIRS = 2
DELTA_BLK = 2 * DN_CHUNK
DELTA_ROWS = DELTA_GROUP * DN_CHUNK
CHUNK_SHIFT = DN_CHUNK.bit_length() - 1
PAIR_W = 2 * DN_DV
DELTA_SPILL_BYTES = 14 * 1024 * 1024


def _block_diag2(x):
    r = x.shape[0]
    z = jnp.zeros((r, r), x.dtype)
    return jnp.concatenate([jnp.concatenate([x[:, :r], z], axis=1),
                            jnp.concatenate([z, x[:, r:]], axis=1)], axis=0)


def _delta_masks():
    n, c = DELTA_BLK, DN_CHUNK
    ii = lax.broadcasted_iota(jnp.int32, (n, n), 0)
    jj = lax.broadcasted_iota(jnp.int32, (n, n), 1)
    same = jnp.right_shift(ii, CHUNK_SHIFT) == jnp.right_shift(jj, CHUNK_SHIFT)
    ge = jnp.logical_and(same, ii >= jj)
    gt = jnp.logical_and(same, ii > jj)
    le = jnp.logical_and(same, ii <= jj)
    lt = jnp.logical_and(same, ii < jj)
    top = lax.broadcasted_iota(jnp.int32, (n, 1), 0) < c
    r4 = lax.broadcasted_iota(jnp.int32, (c, 4 * c), 0)
    l4 = lax.broadcasted_iota(jnp.int32, (c, 4 * c), 1)
    eye4 = jnp.where(r4 == jnp.bitwise_and(l4, c - 1), 1.0, 0.0).astype(F32)
    rb = lax.broadcasted_iota(jnp.int32, (4 * c, 4 * c), 0)
    lb = lax.broadcasted_iota(jnp.int32, (4 * c, 4 * c), 1)
    bd4 = jnp.right_shift(rb, CHUNK_SHIFT) == jnp.right_shift(lb, CHUNK_SHIFT)
    left = lax.broadcasted_iota(jnp.int32, (c, n), 1) < c
    return dict(eye=ii == jj, eye4=eye4, bd4=bd4, left=left, top=top,
                incl=(ge, le), strict=(gt, lt), cum=(ge, le))


def _block_diag4(x, mk):
    return jnp.where(mk["bd4"], jnp.concatenate([x] * 4, axis=0), jnp.zeros((), x.dtype))


def _delta_local(q_ref, k_ref, v_ref, bg_ref, bases, mk):
    c, n, dv = DN_CHUNK, DELTA_BLK, DN_DV
    units = [(pp, d, b) for pp in range(DELTA_PAIRS) for d in range(2) for b in range(DELTA_GROUP // 2)]
    st = {u: {} for u in units}
    neg_inf = jnp.float32(-jnp.inf)

    def stage_kq():
        for pp, d, b in units:
            s = st[(pp, d, b)]
            rs = pl.ds(pl.multiple_of((bases[d] + b) * n, n), n)
            kd = k_ref[0, rs, pp * DN_DK:(pp + 1) * DN_DK]
            qd = q_ref[0, rs, pp * DN_DK:(pp + 1) * DN_DK]
            kq = _dot_nt(jnp.concatenate([kd, qd], axis=0), kd)
            kk = kq[:n]
            qk = kq[n:]
            kf = kd.astype(F32)
            qf = qd.astype(F32)
            bg = bg_ref[bases[d] + b]
            a_list = []
            for h in range(2):
                row = 8 * pp + 2 * d + h
                brow = bg[row:row + 1, :]
                grow = bg[row + 4:row + 5, :]
                gc_col = jnp.sum(jnp.where(mk["cum"][d], grow, 0.0), axis=1, keepdims=True)
                gc_row = jnp.sum(jnp.where(mk["eye"], gc_col, 0.0), axis=0, keepdims=True)
                bcol = jnp.sum(jnp.where(mk["eye"], brow, 0.0), axis=1, keepdims=True)
                dec = jnp.exp(jnp.where(mk["incl"][d], gc_col - gc_row, neg_inf))
                a = jnp.where(mk["strict"][d], kk * bcol * dec, 0.0)
                a_list.append(a[:c] + a[c:])
                gl = (gc_row[:, c - 1:c], gc_row[:, n - 1:n]) if d == 0 else (gc_row[:, 0:1], gc_row[:, c:c + 1])
                gl_col = jnp.where(mk["top"], gl[0], gl[1])
                e1 = jnp.exp(gc_col)
                e2 = jnp.exp(gl_col - gc_col)
                vf = v_ref[0, rs, (2 * pp + h) * dv:(2 * pp + h + 1) * dv].astype(F32)
                s[h] = dict(rhs=jnp.concatenate([vf * bcol, kf * (bcol * e1)], axis=1).astype(BF16),
                            qe=(qf * e1).astype(BF16), e2=jnp.broadcast_to(e2, (n, dv)),
                            qkd=(qk * dec).astype(BF16), eg=[jnp.exp(gl[0]), jnp.exp(gl[1])])
            s["x"] = -jnp.concatenate(a_list, axis=1)
            s["q"] = mk["eye4"] + s["x"]

    def stage_t_first():
        for u in units:
            s = st[u]
            xb = s["x"].astype(BF16)
            s["x"] = _bdot(xb, _block_diag4(xb, mk))

    def stage_t_mid():
        for u in units:
            s = st[u]
            xb = s["x"].astype(BF16)
            y = _bdot(jnp.concatenate([s["q"].astype(BF16), xb], axis=0), _block_diag4(xb, mk))
            s["q"] = s["q"] + y[:c]
            s["x"] = y[c:]

    def stage_t_last():
        for u in units:
            s = st[u]
            s["q"] = s["q"] + _bdot(s["q"].astype(BF16), _block_diag4(s["x"].astype(BF16), mk))

    def stage_uw():
        for u in units:
            s = st[u]
            for h in range(2):
                th = s["q"][:, h * n:(h + 1) * n].astype(BF16)
                zero = jnp.zeros((), BF16)
                t_bd = jnp.concatenate([jnp.where(mk["left"], th, zero), jnp.where(mk["left"], zero, th)], axis=0)
                uw = _bdot(t_bd, s[h]["rhs"])
                s[h]["u"] = uw[:, :dv]
                s[h]["w"] = uw[:, dv:].astype(BF16)

    stages = [stage_kq, stage_t_first] + [stage_t_mid] * (CHUNK_SHIFT - 2) + [stage_t_last, stage_uw]
    return stages, st


def _delta_store_local(st, slot, u_s, w_s, qe_s, e2_s, qkd_s, eg_s):
    for pp in range(DELTA_PAIRS):
        for d in range(2):
            ch = slot * (2 * DELTA_PAIRS) + 2 * pp + d
            blocks = [st[(pp, d, b)] for b in range(DELTA_GROUP // 2)]
            for name, ref in (("u", u_s), ("w", w_s), ("qe", qe_s), ("e2", e2_s), ("qkd", qkd_s)):
                ref[ch] = jnp.concatenate(
                    [jnp.concatenate([blk[0][name], blk[1][name]], axis=1) for blk in blocks], axis=0)
            for b, blk in enumerate(blocks):
                for half in range(2):
                    j = 2 * b + half
                    eg_s[ch, j:j + 1, :] = jnp.concatenate(
                        [jnp.broadcast_to(blk[h]["eg"][half], (1, DN_DV)) for h in range(2)], axis=1)


def _delta_scan(k_ref, slot, u_s, w_s, qe_s, e2_s, qkd_s, eg_s, s_ref, o_ref, acc_ref, bases):
    c, dv = DN_CHUNK, DN_DV
    n_chains = 2 * DELTA_PAIRS
    first = slot * n_chains
    stages = []
    for jf in range(DELTA_GROUP):
        wq = {}

        def chunk_of(ch, jf=jf):
            return jf if ch % 2 == 0 else DELTA_GROUP - 1 - jf

        def stage1(wq=wq, chunk_of=chunk_of):
            for ch in range(n_chains):
                j = chunk_of(ch)
                rl = slice(j * c, (j + 1) * c)
                lhs = jnp.concatenate([w_s[first + ch, rl, :], qe_s[first + ch, rl, :]], axis=0)
                wq[ch] = _bdot(lhs, _block_diag2(s_ref[ch].astype(BF16)))

        def stage2(wq=wq, chunk_of=chunk_of):
            zpad = jnp.zeros((c, dv), BF16)
            for ch in range(n_chains):
                pp, d = ch // 2, ch % 2
                j = chunk_of(ch)
                rl = slice(j * c, (j + 1) * c)
                r0 = pl.multiple_of(bases[d] * DELTA_BLK + j * c, c)
                v_new = u_s[first + ch, rl, :] - wq[ch][:c]
                vb = v_new.astype(BF16)
                pads = []
                for h in range(2):
                    vh = vb[:, h * dv:(h + 1) * dv]
                    pads.append(jnp.concatenate([vh, zpad] if j % 2 == 0 else [zpad, vh], axis=0))
                v_bd = _block_diag2(jnp.concatenate(pads, axis=1))
                o = wq[ch][c:] + _bdot(qkd_s[first + ch, rl, :], v_bd)
                kc = k_ref[0, pl.ds(r0, c), pp * DN_DK:(pp + 1) * DN_DK]
                decay = eg_s[first + ch, j:j + 1, :]
                s_ref[ch] = s_ref[ch] * decay + _dot_tn(kc, (v_new * e2_s[first + ch, rl, :]).astype(BF16))
                if d == 0:
                    o_ref[0, pl.ds(r0, c), pp * PAIR_W:(pp + 1) * PAIR_W] = o.astype(o_ref.dtype)
                else:
                    acc_ref[pp, pl.ds(r0, c), :] = o.astype(acc_ref.dtype)

        stages += [stage1, stage2]
    return stages


def _delta_kernel(q_ref, k_ref, v_ref, ba_ref, alog_ref, dtb_ref, o_ref,
                  bg_ref, s_ref, acc_ref, u_s, w_s, qe_s, e2_s, qkd_s, eg_s):
    seq = q_ref.shape[1]
    n_groups = seq // DELTA_ROWS
    n = DELTA_BLK

    raw = ba_ref[0, 0]
    xx = raw + dtb_ref[0]
    softplus = jnp.maximum(xx, 0.0) + jnp.log(1.0 + jnp.exp(-jnp.abs(xx)))
    g = -jnp.exp(alog_ref[0]) * softplus
    is_beta = jnp.bitwise_and(lax.broadcasted_iota(jnp.int32, (raw.shape[0], 1), 0), 7) < 4
    bg = jnp.where(is_beta, jax.nn.sigmoid(raw), g)
    for i in range(seq // n):
        bg_ref[i] = bg[:, i * n:(i + 1) * n]
    s_ref[...] = jnp.zeros(s_ref.shape, F32)

    mk = _delta_masks()
    local_scratch = (u_s, w_s, qe_s, e2_s, qkd_s, eg_s)

    def group_bases(t):
        return (t * (DELTA_GROUP // 2), (n_groups - 1 - t) * (DELTA_GROUP // 2))

    stages, st = _delta_local(q_ref, k_ref, v_ref, bg_ref, group_bases(0), mk)
    for stage in stages:
        stage()
    _delta_store_local(st, 0, *local_scratch)

    def trip(t, carry):
        slot = jnp.bitwise_and(t, 1)
        l_stages, l_st = _delta_local(q_ref, k_ref, v_ref, bg_ref, group_bases(t + 1), mk)
        s_stages = _delta_scan(k_ref, slot, *local_scratch, s_ref, o_ref, acc_ref, group_bases(t))
        for a, b in zip(l_stages, s_stages):
            b()
            a()
        _delta_store_local(l_st, 1 - slot, *local_scratch)
        return carry

    lax.fori_loop(0, n_groups - 1, trip, 0)
    last = n_groups - 1
    for stage in _delta_scan(k_ref, last % 2, *local_scratch, s_ref, o_ref, acc_ref, group_bases(last)):
        stage()

    rows = min(512, seq)
    for r in range(0, seq, rows):
        for pp in range(DELTA_PAIRS):
            cols = slice(pp * PAIR_W, (pp + 1) * PAIR_W)
            o = o_ref[0, r:r + rows, cols].astype(F32) + acc_ref[pp, r:r + rows, :].astype(F32)
            o_ref[0, r:r + rows, cols] = o.astype(o_ref.dtype)


def _dn_delta(qkv_n, ba_r, alog_r, dtb_r):
    b, seq, _ = qkv_n.shape
    assert seq % DELTA_ROWS == 0 and len(_delta_local(None, None, None, None, None, None)[0]) == 2 * DELTA_GROUP
    n_steps = DN_NK // DELTA_PAIRS
    qk_w = DELTA_PAIRS * DN_DK
    v_w = DELTA_PAIRS * PAIR_W
    k_tile0 = DN_K_DIM // qk_w
    v_tile0 = 2 * DN_K_DIM // v_w
    n_rows = 8 * DELTA_PAIRS
    n_chains = 2 * DELTA_PAIRS
    grp = lambda dt: pltpu.VMEM((2 * n_chains, DELTA_ROWS, PAIR_W), dt)
    block_bytes = 2 * (2 * seq * qk_w * 2 + 2 * seq * v_w * 2 + n_rows * seq * 4)
    scratch_bytes = (DELTA_PAIRS * seq * PAIR_W * 2 + 2 * n_chains * DELTA_ROWS * PAIR_W * (4 + 2 + 2 + 4 + 2)
                     + n_chains * DN_DK * PAIR_W * 4 + n_rows * seq * 4 + 2 * n_chains * 8 * PAIR_W * 4)
    vmem = block_bytes + scratch_bytes + DELTA_SPILL_BYTES
    return pl.pallas_call(
        _delta_kernel,
        grid=(b, n_steps),
        in_specs=[pl.BlockSpec((1, seq, qk_w), lambda i, p: (i, 0, p)),
                  pl.BlockSpec((1, seq, qk_w), lambda i, p: (i, 0, k_tile0 + p)),
                  pl.BlockSpec((1, seq, v_w), lambda i, p: (i, 0, v_tile0 + p)),
                  pl.BlockSpec((1, 1, n_rows, seq), lambda i, p: (i, p, 0, 0)),
                  pl.BlockSpec((1, n_rows, 1), lambda i, p: (p, 0, 0)),
                  pl.BlockSpec((1, n_rows, 1), lambda i, p: (p, 0, 0))],
        out_specs=pl.BlockSpec((1, seq, v_w), lambda i, p: (i, 0, p)),
        out_shape=jax.ShapeDtypeStruct((b, seq, DN_V_DIM), BF16),
        scratch_shapes=[pltpu.VMEM((seq // DELTA_BLK, n_rows, DELTA_BLK), F32),
                        pltpu.VMEM((n_chains, DN_DK, PAIR_W), F32),
                        pltpu.VMEM((DELTA_PAIRS, seq, PAIR_W), BF16),
                        grp(F32), grp(BF16), grp(BF16), grp(F32), grp(BF16),
                        pltpu.VMEM((2 * n_chains, 8, PAIR_W), F32)],
        compiler_params=_cparams("parallel", "parallel", vmem=vmem),
        name="dn_delta",
    )(qkv_n, qkv_n, qkv_n, ba_r, alog_r, dtb_r)


def _step_rows(t):
    n_steps = DN_NK // DELTA_PAIRS
    t = t.astype(F32).reshape(2, n_steps, DELTA_PAIRS, 2).transpose(1, 2, 0, 3).reshape(n_steps, DELTA_PAIRS, 4)
    return jnp.concatenate([jnp.zeros_like(t), t], axis=2).reshape(n_steps, 8 * DELTA_PAIRS, 1)


def _gated_proj_ln_kernel(o_ref, z_ref, nw_ref, w_ref, x_ref, g_ref, b_ref, out_ref):
    nw = nw_ref[...]
    cols = []
    for h in range(DN_NV):
        sl = slice(h * DN_DV, (h + 1) * DN_DV)
        o = o_ref[:, sl].astype(F32)
        ms = jnp.mean(o * o, axis=-1, keepdims=True)
        cols.append((o * lax.rsqrt(ms + RMS_EPS) * nw * _silu(z_ref[:, sl].astype(F32))).astype(BF16))
    y = _bdot(jnp.concatenate(cols, axis=1), w_ref[...])
    out_ref[...] = _layernorm(DEEPNORM_ALPHA * x_ref[...] + y, g_ref[...], b_ref[...])


def _gated_proj_ln(o, qkvz, norm_w, w, x, g, b, tm=512):
    t, k = o.shape
    d = w.shape[1]
    tm = min(tm, t)
    z_tile = DN_CONV_CH // k
    return pl.pallas_call(
        _gated_proj_ln_kernel,
        grid=(t // tm,),
        in_specs=[pl.BlockSpec((tm, k), lambda i: (i, 0)),
                  pl.BlockSpec((tm, k), lambda i: (i, z_tile)),
                  pl.BlockSpec((1, DN_DV), lambda i: (0, 0)),
                  pl.BlockSpec((k, d), lambda i: (0, 0)),
                  pl.BlockSpec((tm, d), lambda i: (i, 0)),
                  pl.BlockSpec((1, d), lambda i: (0, 0)),
                  pl.BlockSpec((1, d), lambda i: (0, 0))],
        out_specs=pl.BlockSpec((tm, d), lambda i: (i, 0)),
        out_shape=jax.ShapeDtypeStruct((t, d), F32),
        compiler_params=_cparams("parallel"),
        name="gated_proj_ln",
    )(o, qkvz, norm_w.reshape(1, DN_DV), w, x, g.reshape(1, d), b.reshape(1, d))


def _deltanet_layer(xf, bsz, seq, w_in, conv_w, a_log, dt_bias, norm_w, w_out, ln_g, ln_b):
    t = xf.shape[0]
    n_qkvz = DN_CONV_CH + DN_V_DIM
    n_steps = DN_NK // DELTA_PAIRS
    w_bf = w_in.astype(BF16)
    qkvz = _matmul(xf, w_bf[:, :n_qkvz], BF16, tm=1024, tn=1536)
    ba = _matmul(xf, w_bf[:, n_qkvz:], F32, tm=1024, tn=4 * DN_NV)
    ba_r = ba.reshape(bsz, seq, 2, 2, n_steps, DELTA_PAIRS, 2).transpose(0, 4, 5, 2, 3, 6, 1)
    ba_r = ba_r.reshape(bsz, n_steps, 8 * DELTA_PAIRS, seq)
    qkv_n = _dn_conv(qkvz.reshape(bsz, seq, n_qkvz), conv_w)
    o = _dn_delta(qkv_n, ba_r, _step_rows(a_log), _step_rows(dt_bias))
    return _gated_proj_ln(o.reshape(t, DN_V_DIM), qkvz, norm_w, w_out.astype(BF16), xf, ln_g, ln_b)


MLA_QK = MLA_NOPE + MLA_ROPE
MLA_IN_EXT = MLA_QR + MLA_KVR + 2 * LANES
MLA_Q_EXT = MLA_NOPE + 2 * MLA_ROPE
MLA_V_EXT = MLA_V + 16
LOG2E = 1.4426950408889634


def _rope_kernel(pos_ref, invf_ref, sign_ref, cs_ref, sn_ref):
    ang = invf_ref[...] * pos_ref[0]
    cs_ref[0] = jnp.cos(ang)
    sn_ref[0] = jnp.sin(ang) * sign_ref[...]


def _rope_tables(positions):
    b, seq = positions.shape
    half = MLA_ROPE // 2
    inv_freq = 1.0 / (MLA_ROPE_THETA ** (jnp.arange(0, MLA_ROPE, 2, dtype=F32) / MLA_ROPE))
    invf = jnp.concatenate([inv_freq, inv_freq]).reshape(MLA_ROPE, 1)
    sign = jnp.concatenate([-jnp.ones((half,), F32), jnp.ones((half,), F32)]).reshape(MLA_ROPE, 1)
    out = jax.ShapeDtypeStruct((b, MLA_ROPE, seq), F32)
    return pl.pallas_call(
        _rope_kernel,
        grid=(b,),
        in_specs=[pl.BlockSpec((1, 1, seq), lambda i: (i, 0, 0)),
                  pl.BlockSpec((MLA_ROPE, 1), lambda i: (0, 0)),
                  pl.BlockSpec((MLA_ROPE, 1), lambda i: (0, 0))],
        out_specs=[pl.BlockSpec((1, MLA_ROPE, seq), lambda i: (i, 0, 0)),
                   pl.BlockSpec((1, MLA_ROPE, seq), lambda i: (i, 0, 0))],
        out_shape=[out, out],
        compiler_params=_cparams("parallel"),
        name="rope_tables",
    )(positions.astype(F32).reshape(b, 1, seq), invf, sign)


def _rms(x, w):
    return x * lax.rsqrt(jnp.mean(x * x, axis=-1, keepdims=True) + RMS_EPS) * w


def _mla_in_kernel(x_ref, w_ref, qn_ref, kvn_ref, cs_ref, sn_ref, cq_ref, ckv_ref, kr_ref):
    h = _bdot(x_ref[...].astype(BF16), w_ref[...])
    cq_ref[...] = _rms(h[:, :MLA_QR], qn_ref[...]).astype(cq_ref.dtype)
    ckv_ref[...] = _rms(h[:, MLA_QR:MLA_QR + MLA_KVR], kvn_ref[...]).astype(ckv_ref.dtype)
    r0 = MLA_QR + MLA_KVR
    kr = h[:, r0:r0 + MLA_ROPE] * cs_ref[...] + h[:, r0 + LANES:r0 + LANES + MLA_ROPE] * sn_ref[...]
    kr_ref[...] = kr.astype(kr_ref.dtype)


def _mla_in(xf, w_ext, q_norm, kv_norm, cs, sn, tm=512):
    t, d = xf.shape
    tm = min(tm, t)
    row = lambda i: (i, 0)
    fix = lambda i: (0, 0)
    return pl.pallas_call(
        _mla_in_kernel,
        grid=(t // tm,),
        in_specs=[pl.BlockSpec((tm, d), row),
                  pl.BlockSpec((d, MLA_IN_EXT), fix),
                  pl.BlockSpec((1, MLA_QR), fix),
                  pl.BlockSpec((1, MLA_KVR), fix),
                  pl.BlockSpec((tm, MLA_ROPE), row),
                  pl.BlockSpec((tm, MLA_ROPE), row)],
        out_specs=[pl.BlockSpec((tm, MLA_QR), row),
                   pl.BlockSpec((tm, MLA_KVR), row),
                   pl.BlockSpec((tm, MLA_ROPE), row)],
        out_shape=[jax.ShapeDtypeStruct((t, MLA_QR), BF16),
                   jax.ShapeDtypeStruct((t, MLA_KVR), BF16),
                   jax.ShapeDtypeStruct((t, MLA_ROPE), BF16)],
        compiler_params=_cparams("parallel"),
        name="mla_in",
    )(xf, w_ext, q_norm.reshape(1, MLA_QR), kv_norm.reshape(1, MLA_KVR), cs, sn)


def _mla_q_kernel(cq_ref, w_ref, cs_ref, sn_ref, q_ref):
    acc = _dot_nt(w_ref[...], cq_ref[...])
    cs = cs_ref[0]
    sn = sn_ref[0]
    scale = MLA_QK ** -0.5 * LOG2E
    for h in range(MLA_H):
        a0 = h * MLA_Q_EXT
        o0 = h * MLA_QK
        q_ref[0, o0:o0 + MLA_NOPE, :] = (acc[a0:a0 + MLA_NOPE] * scale).astype(q_ref.dtype)
        x = acc[a0 + MLA_NOPE:a0 + MLA_NOPE + MLA_ROPE]
        xs = acc[a0 + MLA_NOPE + MLA_ROPE:a0 + MLA_Q_EXT]
        q_ref[0, o0 + MLA_NOPE:o0 + MLA_QK, :] = ((x * cs + xs * sn) * scale).astype(q_ref.dtype)


def _mla_q(cq, wq_t, cs_t, sn_t, bsz, seq, tl=512):
    tl = min(tl, seq)
    nl = seq // tl
    return pl.pallas_call(
        _mla_q_kernel,
        grid=(bsz, nl),
        in_specs=[pl.BlockSpec((tl, MLA_QR), lambda b, i: (b * nl + i, 0)),
                  pl.BlockSpec((MLA_H * MLA_Q_EXT, MLA_QR), lambda b, i: (0, 0)),
                  pl.BlockSpec((1, MLA_ROPE, tl), lambda b, i: (b, 0, i)),
                  pl.BlockSpec((1, MLA_ROPE, tl), lambda b, i: (b, 0, i))],
        out_specs=pl.BlockSpec((1, MLA_H * MLA_QK, tl), lambda b, i: (b, 0, i)),
        out_shape=jax.ShapeDtypeStruct((bsz, MLA_H * MLA_QK, seq), BF16),
        compiler_params=_cparams("parallel", "parallel"),
        name="mla_q",
    )(cq, wq_t, cs_t, sn_t)


def _mla_kv_kernel(ckv_ref, kr_ref, wk_ref, wv_ref, k_ref, v_ref):
    ckv = ckv_ref[...]
    kn = _bdot(ckv, wk_ref[...])
    kr = kr_ref[...]
    for h in range(MLA_H):
        k_ref[0, h] = jnp.concatenate(
            [kn[:, h * MLA_NOPE:(h + 1) * MLA_NOPE].astype(k_ref.dtype), kr], axis=1)
    v_t = _dot_nt(wv_ref[...], ckv).astype(v_ref.dtype)
    ones = jnp.ones((MLA_V_EXT - MLA_V, v_t.shape[1]), v_ref.dtype)
    for h in range(MLA_H):
        v_ref[0, h * MLA_V_EXT:h * MLA_V_EXT + MLA_V, :] = v_t[h * MLA_V:(h + 1) * MLA_V]
        v_ref[0, h * MLA_V_EXT + MLA_V:(h + 1) * MLA_V_EXT, :] = ones


def _mla_kv(ckv, kr, wk, wv_t, bsz, seq, tl=512):
    tl = min(tl, seq)
    nl = seq // tl
    return pl.pallas_call(
        _mla_kv_kernel,
        grid=(bsz, nl),
        in_specs=[pl.BlockSpec((tl, MLA_KVR), lambda b, i: (b * nl + i, 0)),
                  pl.BlockSpec((tl, MLA_ROPE), lambda b, i: (b * nl + i, 0)),
                  pl.BlockSpec((MLA_KVR, MLA_H * MLA_NOPE), lambda b, i: (0, 0)),
                  pl.BlockSpec((MLA_H * MLA_V, MLA_KVR), lambda b, i: (0, 0))],
        out_specs=[pl.BlockSpec((1, MLA_H, tl, MLA_QK), lambda b, i: (b, 0, i, 0)),
                   pl.BlockSpec((1, MLA_H * MLA_V_EXT, tl), lambda b, i: (b, 0, i))],
        out_shape=[jax.ShapeDtypeStruct((bsz, MLA_H, seq, MLA_QK), BF16),
                   jax.ShapeDtypeStruct((bsz, MLA_H * MLA_V_EXT, seq), BF16)],
        compiler_params=_cparams("parallel", "parallel"),
        name="mla_kv",
    )(ckv, kr, wk, wv_t)


ATTN_TQ = 2048
ATTN_TK = 512


def _attn_kernel(q_ref, k_ref, v_ref, o_ref):
    q_t = q_ref[0]
    tq = q_t.shape[1]
    seq = k_ref.shape[2]
    tk = min(ATTN_TK, seq)
    m = jnp.full((1, tq), -jnp.inf, F32)
    acc = jnp.zeros((MLA_V_EXT, tq), F32)
    n_k = seq // tk
    scores = lambda j: _bdot(k_ref[0, 0, j * tk:(j + 1) * tk, :], q_t)
    s_next = scores(0)
    for j in range(n_k):
        s = s_next
        if j + 1 < n_k:
            s_next = scores(j + 1)
        m_new = jnp.maximum(m, jnp.max(s, axis=0, keepdims=True))
        p = jnp.exp2(s - m_new).astype(BF16)
        acc = jnp.exp2(m - m_new) * acc + _bdot(v_ref[0, :, j * tk:(j + 1) * tk], p)
        m = m_new
    o_ref[0] = (acc[:MLA_V] / acc[MLA_V:MLA_V + 1]).T.astype(o_ref.dtype)


def _mla_attn(q_t, k, v_t, bsz, seq):
    tq = min(ATTN_TQ, seq)
    return pl.pallas_call(
        _attn_kernel,
        grid=(bsz, MLA_H, seq // tq),
        in_specs=[pl.BlockSpec((1, MLA_QK, tq), lambda b, h, i: (b, h, i)),
                  pl.BlockSpec((1, 1, seq, MLA_QK), lambda b, h, i: (b, h, 0, 0)),
                  pl.BlockSpec((1, MLA_V_EXT, seq), lambda b, h, i: (b, h, 0))],
        out_specs=pl.BlockSpec((1, tq, MLA_V), lambda b, h, i: (b, i, h)),
        out_shape=jax.ShapeDtypeStruct((bsz, seq, MLA_H * MLA_V), BF16),
        compiler_params=_cparams("parallel", "parallel", "parallel"),
        name="mla_attn",
    )(q_t, k, v_t)


def _swap_halves(t):
    half = t.shape[-1] // 2
    return jnp.concatenate([t[..., half:], t[..., :half]], axis=-1)


def _mla_layer(xf, bsz, seq, cs, sn, cs_t, sn_t, w_in, q_norm, w_qb, kv_norm, w_kvb, w_out, ln_g, ln_b):
    t, d = xf.shape
    r0 = MLA_QR + MLA_KVR
    w_rope = w_in[:, r0:]
    pad = jnp.zeros((d, LANES - MLA_ROPE), w_in.dtype)
    w_ext = jnp.concatenate([w_in, pad, _swap_halves(w_rope), pad], axis=1).astype(BF16)
    wq = w_qb.reshape(MLA_QR, MLA_H, MLA_QK)
    wq_rope = wq[:, :, MLA_NOPE:]
    wq_t = jnp.concatenate([wq, _swap_halves(wq_rope)], axis=-1).reshape(MLA_QR, MLA_H * MLA_Q_EXT).T.astype(BF16)
    wkv = w_kvb.reshape(MLA_KVR, MLA_H, MLA_NOPE + MLA_V)
    wk = wkv[:, :, :MLA_NOPE].reshape(MLA_KVR, MLA_H * MLA_NOPE).astype(BF16)
    wv_t = wkv[:, :, MLA_NOPE:].reshape(MLA_KVR, MLA_H * MLA_V).T.astype(BF16)

    cq, ckv, kr = _mla_in(xf, w_ext, q_norm, kv_norm, cs, sn)
    q_t = _mla_q(cq, wq_t, cs_t, sn_t, bsz, seq)
    k, v_t = _mla_kv(ckv, kr, wk, wv_t, bsz, seq)
    o = _mla_attn(q_t, k, v_t, bsz, seq)
    return _proj_ln(o.reshape(t, MLA_H * MLA_V), w_out.astype(BF16), xf, ln_g, ln_b)


def _ind(mask):
    return jnp.where(mask, 1.0, 0.0)


def _route_rows(logits_t, bias_col):
    scores = jax.nn.sigmoid(logits_t)
    sel = scores + bias_col
    s = [sel[e:e + 1] for e in range(N_EXPERTS)]
    sc = [scores[e:e + 1] for e in range(N_EXPERTS)]

    def ranks(vals):
        rank = [jnp.zeros_like(vals[0]) for _ in vals]
        for i in range(len(vals)):
            for k in range(i + 1, len(vals)):
                k_wins = _ind(vals[k] > vals[i])
                rank[i] = rank[i] + k_wins
                rank[k] = rank[k] + (1.0 - k_wins)
        return rank

    top2, gsum = [], []
    for g in range(N_GROUPS):
        idx = range(g * EXPERTS_PER_GROUP, (g + 1) * EXPERTS_PER_GROUP)
        rk = ranks([s[e] for e in idx])
        t2 = [r < 2.0 for r in rk]
        top2.append(t2)
        gsum.append(sum(jnp.where(t, s[e], 0.0) for t, e in zip(t2, idx)))
    won = [r < 0.5 for r in ranks(gsum)]
    w = [sum(jnp.where(won[g], jnp.where(top2[g][e], sc[g * EXPERTS_PER_GROUP + e], 0.0), 0.0)
             for g in range(N_GROUPS)) for e in range(EXPERTS_PER_GROUP)]
    tot = sum(w)
    return [we / tot for we in w], [_ind(m) for m in won]


MOE_BLOCK = 128
GATE_ROWS = 8


def _router_weights(router_w):
    hi, lo = _split_bf16(router_w.astype(F32).T)
    return jnp.concatenate([hi, lo], axis=0)


def _split_bf16(v):
    hi = v.astype(BF16)
    return hi, (v - hi.astype(F32)).astype(BF16)


def _rows8(rows):
    pad = [jnp.zeros_like(rows[0])] * (GATE_ROWS - len(rows))
    return jnp.concatenate(list(rows) + pad, axis=0)


def _moe_kernel(x_ref, rw_ref, rb_ref, wg_ref, wu_ref, wd_ref, g_ref, b_ref, o_ref,
                xs_ref, pt_ref, gate_ref, acc_ref, seg_ref):
    j = pl.program_id(1)
    tm = x_ref.shape[0]

    @pl.when(j == 0)
    def _():
        x = x_ref[...]
        xh, xl = _split_bf16(x)
        rw = rw_ref[...]
        lg = _dot_nt(rw, xh)
        logits_t = lg[:N_EXPERTS] + lg[N_EXPERTS:] + _dot_nt(rw[:N_EXPERTS], xl)
        gates, member = _route_rows(logits_t, rb_ref[...])
        ri = lax.broadcasted_iota(jnp.int32, (tm, tm), 0)
        ci = lax.broadcasted_iota(jnp.int32, (tm, tm), 1)
        utri = jnp.where(ri <= ci, 1.0, 0.0).astype(BF16)
        prefix = _bdot(_rows8(member).astype(BF16), utri)
        cnt = [prefix[g:g + 1, tm - 1:tm] for g in range(N_GROUPS)]
        off = [jnp.zeros((1, 1), F32)]
        for g in range(1, N_GROUPS):
            off.append(off[-1] + cnt[g - 1])
        pos_row = sum(member[g] * (off[g] + prefix[g:g + 1] - 1.0) for g in range(N_GROUPS))
        pos_col = jnp.sum(jnp.where(ri == ci, pos_row, 0.0), axis=1, keepdims=True)
        p = jnp.where(pos_row == ri.astype(F32), 1.0, 0.0).astype(BF16)
        pt_ref[...] = jnp.where(pos_col == ci.astype(F32), 1.0, 0.0).astype(BF16)
        xs_ref[...] = _bdot(p, xh).astype(BF16)
        gh, gl = _split_bf16(_rows8(gates))
        gate_ref[...] = _dot_nt(p, gh) + _dot_nt(p, gl)
        for g in range(N_GROUPS):
            seg_ref[g] = off[g][0, 0].astype(jnp.int32)
            seg_ref[N_GROUPS + g] = cnt[g][0, 0].astype(jnp.int32)
        acc_ref[...] = jnp.zeros(acc_ref.shape, F32)

    seg_lo = seg_ref[j]
    seg_hi = seg_lo + seg_ref[N_GROUPS + j]
    for r in range(0, tm, MOE_BLOCK):
        @pl.when(jnp.logical_and(seg_lo < r + MOE_BLOCK, seg_hi > r))
        def _(r=r):
            xb = xs_ref[r:r + MOE_BLOCK, :]
            row = r + lax.broadcasted_iota(jnp.int32, (MOE_BLOCK, 1), 0)
            mine = jnp.logical_and(row >= seg_lo, row < seg_hi)
            gates = jnp.where(mine, gate_ref[r:r + MOE_BLOCK, :], 0.0)
            hid = []
            for e in range(EXPERTS_PER_GROUP):
                hg = _bdot(xb, wg_ref[e])
                hu = _bdot(xb, wu_ref[e])
                hid.append((_silu(hg) * hu * gates[:, e:e + 1]).astype(BF16))
            acc_ref[r:r + MOE_BLOCK, :] += _bdot(jnp.concatenate(hid, axis=1), wd_ref[...])

    @pl.when(j == N_GROUPS - 1)
    def _():
        y = _bdot(pt_ref[...], acc_ref[...].astype(BF16))
        o_ref[...] = _layernorm(DEEPNORM_ALPHA * x_ref[...] + y, g_ref[...], b_ref[...])


def _moe_layer(xf, rw_t, rb_t, w_gate, w_up, w_down, ln_g, ln_b, tm=512):
    t, d = xf.shape
    tm = min(tm, t)
    f = D_FF_EXPERT
    gf = EXPERTS_PER_GROUP * f
    row = lambda i, j: (i, 0)
    fix = lambda i, j: (0, 0)
    return pl.pallas_call(
        _moe_kernel,
        grid=(t // tm, N_GROUPS),
        in_specs=[pl.BlockSpec((tm, d), row),
                  pl.BlockSpec((2 * N_EXPERTS, d), fix),
                  pl.BlockSpec((N_EXPERTS, 1), fix),
                  pl.BlockSpec((EXPERTS_PER_GROUP, d, f), lambda i, j: (j, 0, 0)),
                  pl.BlockSpec((EXPERTS_PER_GROUP, d, f), lambda i, j: (j, 0, 0)),
                  pl.BlockSpec((gf, d), lambda i, j: (j, 0)),
                  pl.BlockSpec((1, d), fix),
                  pl.BlockSpec((1, d), fix)],
        out_specs=pl.BlockSpec((tm, d), row),
        out_shape=jax.ShapeDtypeStruct((t, d), F32),
        scratch_shapes=[pltpu.VMEM((tm, d), BF16),
                        pltpu.VMEM((tm, tm), BF16),
                        pltpu.VMEM((tm, GATE_ROWS), F32),
                        pltpu.VMEM((tm, d), F32),
                        pltpu.SMEM((2 * N_GROUPS,), jnp.int32)],
        compiler_params=_cparams("parallel", "arbitrary"),
        name="moe",
    )(xf, rw_t, rb_t, w_gate.astype(BF16), w_up.astype(BF16),
      w_down.astype(BF16).reshape(N_EXPERTS * f, d), ln_g.reshape(1, d), ln_b.reshape(1, d))


def kernel(x, positions, dn_w_in, dn_conv_w, dn_a_log, dn_dt_bias, dn_norm_w, dn_w_out, mla_w_in, mla_q_norm, mla_w_qb, mla_kv_norm, mla_w_kvb, mla_w_out, ln1_g, ln1_b, ln2_g, ln2_b, router_w, router_bias, moe_w_gate, moe_w_up, moe_w_down):
    bsz, seq, d = x.shape
    t = bsz * seq
    xf = x.reshape(t, d)
    cs_t, sn_t = _rope_tables(positions)
    cs = cs_t.transpose(0, 2, 1).reshape(t, MLA_ROPE)
    sn = sn_t.transpose(0, 2, 1).reshape(t, MLA_ROPE)
    rw_t = _router_weights(router_w)
    rb_t = router_bias.astype(F32).reshape(N_EXPERTS, 1)
    for i in range(DEPTH):
        j = i // 2
        if i % 2 == 0:
            xf = _deltanet_layer(xf, bsz, seq, dn_w_in[j], dn_conv_w[j], dn_a_log[j], dn_dt_bias[j],
                                 dn_norm_w[j], dn_w_out[j], ln1_g[i], ln1_b[i])
        else:
            xf = _mla_layer(xf, bsz, seq, cs, sn, cs_t, sn_t, mla_w_in[j], mla_q_norm[j], mla_w_qb[j],
                            mla_kv_norm[j], mla_w_kvb[j], mla_w_out[j], ln1_g[i], ln1_b[i])
        xf = _moe_layer(xf, rw_t, rb_t, moe_w_gate[i], moe_w_up[i], moe_w_down[i], ln2_g[i], ln2_b[i])
    return xf.reshape(bsz, seq, d)
```
